```python
import math, functools
import jax, jax.numpy as jnp
from jax import lax
import numpy as np

D_MODEL = 2048
BATCH = 2
SEQ = 4096
DEPTH = 1
DEC_BATCH = 32
DEC_SEQ = 8
PAST_LEN = 16384
PAGE_SIZE = 128

PLE_DIM = 256
H_A = 4
DK_A = 128
DV_A = 256
GATE_RANK = 16
GATE_TAU = 16.0
GLA_CHUNK = 64
H_B = 8
DK_B = 128
DV_B = 256
Q_BLOCK = 128
ROPE_THETA = 10000.0
N_GROUPS = 4
EXPERTS_PER_GROUP = 8
N_EXPERTS = 32
D_EXPERT = 256
TOP_K_IN_GROUP = 2
EPS = 1e-6

kernel_name = 'hybrid_gla_diffattn_hmoe_step'


def rmsnorm(x, g):
    xf = x.astype(jnp.float32)
    y = xf * lax.rsqrt(jnp.mean(xf * xf, axis=-1, keepdims=True) + EPS) * g.astype(jnp.float32)
    return y.astype(x.dtype)


def rope(x, pos):
    half = x.shape[-1] // 2
    inv = ROPE_THETA ** (-jnp.arange(half, dtype=jnp.float32) / half)
    ang = pos.astype(jnp.float32)[:, None] * inv[None, :]
    cos = jnp.cos(ang)[None, :, None, None, :]
    sin = jnp.sin(ang)[None, :, None, None, :]
    xf = x.astype(jnp.float32)
    x1, x2 = xf[..., :half], xf[..., half:]
    return jnp.concatenate([x1 * cos - x2 * sin, x2 * cos + x1 * sin], axis=-1).astype(x.dtype)


def gla_chunked(q, k, v, logf, s0, chunk):
    B, T, H, dk = q.shape
    dv = v.shape[-1]
    n = T // chunk

    def blocks(a):
        return a.astype(jnp.float32).reshape(B, n, chunk, H, a.shape[-1]).transpose(1, 0, 3, 2, 4)

    qc, kc, vc, fc = blocks(q), blocks(k), blocks(v), blocks(logf)
    b = jnp.cumsum(fc, axis=3)
    b_last = b[:, :, :, -1:, :]
    q_dec = qc * jnp.exp(b)
    k_dec = kc * jnp.exp(-b)
    k_end = kc * jnp.exp(b_last - b)
    causal = jnp.tril(jnp.ones((chunk, chunk), dtype=bool))
    a_intra = jnp.where(causal, jnp.einsum('nbhcd,nbhsd->nbhcs', q_dec, k_dec), 0.0)
    o_intra = jnp.einsum('nbhcs,nbhsv->nbhcv', a_intra, vc)
    kv_chunk = jnp.einsum('nbhsd,nbhsv->nbhdv', k_end, vc)
    decay = jnp.exp(b_last[:, :, :, 0, :])

    def step(s, xs):
        q_i, decay_i, kv_i = xs
        o_i = jnp.einsum('bhcd,bhdv->bhcv', q_i, s)
        return decay_i[..., None] * s + kv_i, o_i

    s_final, o_inter = lax.scan(step, s0.astype(jnp.float32), (q_dec, decay, kv_chunk))
    o = (o_intra + o_inter).transpose(1, 0, 3, 2, 4).reshape(B, T, H, dv)
    return o.astype(v.dtype), s_final


def diff_weights(s, lam):
    a = jax.nn.softmax(s, axis=-1)
    return a[..., 0, :, :] - lam * a[..., 1, :, :]


def diff_attn_prompt(q, k, v, lam):
    B, S, H, _, dk = q.shape
    dv = v.shape[-1]
    nb = S // Q_BLOCK
    key_pos = jnp.arange(S)
    scale = dk ** -0.5

    def block(i):
        start = i * Q_BLOCK
        qb = lax.dynamic_slice_in_dim(q, start, Q_BLOCK, axis=1)
        s = jnp.einsum('bqhmd,bkhmd->bhmqk', qb, k).astype(jnp.float32) * scale
        q_pos = start + jnp.arange(Q_BLOCK)
        mask = key_pos[None, :] <= q_pos[:, None]
        s = jnp.where(mask, s, -jnp.inf)
        w = diff_weights(s, lam)
        return jnp.einsum('bhqk,bkhv->bqhv', w.astype(v.dtype), v)

    o = lax.map(block, jnp.arange(nb))
    return o.transpose(1, 0, 2, 3, 4).reshape(B, S, H, dv)


def diff_attn_paged(q, k, v, lam, cache_k_l, cache_v_l, page_table):
    T = q.shape[1]
    H, dk = q.shape[2], q.shape[4]
    dv = v.shape[-1]
    scale = dk ** -0.5

    def one(args):
        pt, qs, ks, vs = args
        kp = cache_k_l[pt].reshape(-1, H, 2, dk)
        vp = cache_v_l[pt].reshape(-1, H, dv)
        past = kp.shape[0]
        k_all = jnp.concatenate([kp, ks.astype(kp.dtype)], axis=0)
        v_all = jnp.concatenate([vp, vs.astype(vp.dtype)], axis=0)
        s = jnp.einsum('qhmd,khmd->hmqk', qs, k_all).astype(jnp.float32) * scale
        k_pos = jnp.arange(past + T)
        q_pos = past + jnp.arange(T)
        mask = k_pos[None, :] <= q_pos[:, None]
        s = jnp.where(mask, s, -jnp.inf)
        w = diff_weights(s, lam)
        return jnp.einsum('hqk,khv->qhv', w.astype(v_all.dtype), v_all)

    return lax.map(one, (page_table, q, k, v))


def hier_moe(x, w_rg, b_rg, w_re, b_re, w_g, w_u, w_d):
    lg = (x @ w_rg + b_rg).astype(jnp.float32)
    pg = jax.nn.softmax(lg, axis=-1)
    g_idx = jnp.argmax(lg, axis=-1)
    pg_top = jnp.take_along_axis(pg, g_idx[..., None], axis=-1)
    le = (jnp.einsum('btd,gde->btge', x, w_re) + b_re).astype(jnp.float32)
    le_sel = jnp.take_along_axis(le, g_idx[..., None, None], axis=2)[..., 0, :]
    pe = jax.nn.softmax(le_sel, axis=-1)
    top_p, top_i = lax.top_k(pe, TOP_K_IN_GROUP)
    top_p = top_p / jnp.sum(top_p, axis=-1, keepdims=True) * pg_top
    expert_id = g_idx[..., None] * EXPERTS_PER_GROUP + top_i
    gate = jnp.sum(jax.nn.one_hot(expert_id, N_EXPERTS, dtype=jnp.float32) * top_p[..., None], axis=-2)
    hg = jnp.einsum('btd,xdf->btxf', x, w_g)
    hu = jnp.einsum('btd,xdf->btxf', x, w_u)
    act = jax.nn.silu(hg) * hu * gate[..., None].astype(x.dtype)
    return jnp.einsum('btxf,xfd->btd', act, w_d)


def decoder_layer(h, p_i, pos, s0, gla_chunk, attend, lam_init,
                  norm_mix, w_in, w_alpha2, b_alpha, gla_norm, w_proj_a,
                  lambda_q1, lambda_k1, lambda_q2, lambda_k2, diff_norm, w_proj_b, w_out,
                  norm_ffn, w_router_group, b_router_group, w_router_expert, b_router_expert,
                  w_exp_gate, w_exp_up, w_exp_down, norm_ple, w_ple_gate, w_ple_proj):
    B, T, _ = h.shape
    hn = rmsnorm(h, norm_mix)
    z = hn @ w_in
    sizes = [H_A * DK_A, H_A * DK_A, H_A * DV_A, H_A * DV_A, GATE_RANK,
             H_B * 2 * DK_B, H_B * 2 * DK_B, H_B * DV_B, 2 * D_MODEL]
    qa, ka, va, ra, ga_low, qb, kb, vb, gm = jnp.split(z, list(np.cumsum(sizes)[:-1]), axis=-1)

    qa = qa.reshape(B, T, H_A, DK_A) * (DK_A ** -0.5)
    ka = ka.reshape(B, T, H_A, DK_A)
    va = va.reshape(B, T, H_A, DV_A)
    logf = jax.nn.log_sigmoid((ga_low @ w_alpha2 + b_alpha).astype(jnp.float32)).reshape(B, T, H_A, DK_A) / GATE_TAU
    oa, s_new = gla_chunked(qa, ka, va, logf, s0, gla_chunk)
    oa = rmsnorm(oa, gla_norm) * jax.nn.silu(ra).reshape(B, T, H_A, DV_A)
    ya = oa.reshape(B, T, H_A * DV_A) @ w_proj_a

    qb = rope(qb.reshape(B, T, H_B, 2, DK_B), pos)
    kb = rope(kb.reshape(B, T, H_B, 2, DK_B), pos)
    vb = vb.reshape(B, T, H_B, DV_B)
    lam = (jnp.exp(jnp.sum(lambda_q1.astype(jnp.float32) * lambda_k1.astype(jnp.float32)))
           - jnp.exp(jnp.sum(lambda_q2.astype(jnp.float32) * lambda_k2.astype(jnp.float32))) + lam_init)
    ob = attend(qb, kb, vb, lam)
    ob = rmsnorm(ob, diff_norm) * (1.0 - lam_init)
    yb = ob.reshape(B, T, H_B * DV_B) @ w_proj_b

    g_a, g_b = jnp.split(gm, 2, axis=-1)
    h = h + (jax.nn.sigmoid(g_a) * ya + jax.nn.sigmoid(g_b) * yb) @ w_out

    h = h + hier_moe(rmsnorm(h, norm_ffn), w_router_group, b_router_group, w_router_expert,
                     b_router_expert, w_exp_gate, w_exp_up, w_exp_down)

    gate = jax.nn.sigmoid(rmsnorm(h, norm_ple) @ w_ple_gate)
    h = h + (p_i @ w_ple_proj) * gate
    return h, s_new, kb, vb


def setup_inputs(seed: int = 0) -> dict:
    key = jax.random.key(seed)
    ks = iter(jax.random.split(key, 48))

    def nrm(shape, scale=1.0):
        return jax.random.normal(next(ks), shape, dtype=jnp.float32) * scale

    def gain(shape):
        return 1.0 + nrm(shape, 0.01)

    n_pages = PAST_LEN // PAGE_SIZE
    n_used = DEC_BATCH * n_pages
    n_pool = n_used + n_used // 4
    page_table = jax.random.permutation(next(ks), n_pool)[:n_used].reshape(DEC_BATCH, n_pages).astype(jnp.int32)
    n_in = 4 * H_A * DK_A // 2 * 0 + 2 * H_A * DK_A + 2 * H_A * DV_A + GATE_RANK + 3 * H_B * 2 * DK_B // 3 * 0 + 2 * H_B * 2 * DK_B + H_B * DV_B + 2 * D_MODEL
    D = D_MODEL
    return {
        'x_prompt': nrm((BATCH, SEQ, D)),
        'x_sample': nrm((DEC_BATCH, DEC_SEQ, D)),
        'cache_k': nrm((DEPTH, n_pool, PAGE_SIZE, H_B, 2, DK_B)),
        'cache_v': nrm((DEPTH, n_pool, PAGE_SIZE, H_B, DV_B)),
        'state_gla': nrm((DEPTH, DEC_BATCH, H_A, DK_A, DV_A)),
        'page_table': page_table,
        'p_prompt': nrm((DEPTH, BATCH, SEQ, PLE_DIM)),
        'p_sample': nrm((DEPTH, DEC_BATCH, DEC_SEQ, PLE_DIM)),
        'norm_mix': gain((DEPTH, D)),
        'w_in': nrm((DEPTH, D, n_in), D ** -0.5),
        'w_alpha2': nrm((DEPTH, GATE_RANK, H_A * DK_A), GATE_RANK ** -0.5),
        'b_alpha': nrm((DEPTH, H_A * DK_A), 0.1),
        'gla_norm': gain((DEPTH, DV_A)),
        'w_proj_a': nrm((DEPTH, H_A * DV_A, D), (H_A * DV_A) ** -0.5),
        'lambda_q1': nrm((DEPTH, DK_B), 0.1),
        'lambda_k1': nrm((DEPTH, DK_B), 0.1),
        'lambda_q2': nrm((DEPTH, DK_B), 0.1),
        'lambda_k2': nrm((DEPTH, DK_B), 0.1),
        'diff_norm': gain((DEPTH, DV_B)),
        'w_proj_b': nrm((DEPTH, H_B * DV_B, D), (H_B * DV_B) ** -0.5),
        'w_out': nrm((DEPTH, D, D), D ** -0.5),
        'norm_ffn': gain((DEPTH, D)),
        'w_router_group': nrm((DEPTH, D, N_GROUPS), D ** -0.5),
        'b_router_group': nrm((DEPTH, N_GROUPS), 0.01),
        'w_router_expert': nrm((DEPTH, N_GROUPS, D, EXPERTS_PER_GROUP), D ** -0.5),
        'b_router_expert': nrm((DEPTH, N_GROUPS, EXPERTS_PER_GROUP), 0.01),
        'w_exp_gate': nrm((DEPTH, N_EXPERTS, D, D_EXPERT), D ** -0.5),
        'w_exp_up': nrm((DEPTH, N_EXPERTS, D, D_EXPERT), D ** -0.5),
        'w_exp_down': nrm((DEPTH, N_EXPERTS, D_EXPERT, D), D_EXPERT ** -0.5),
        'norm_ple': gain((DEPTH, D)),
        'w_ple_gate': nrm((DEPTH, D, D), D ** -0.5),
        'w_ple_proj': nrm((DEPTH, PLE_DIM, D), PLE_DIM ** -0.5),
        'norm_final': gain((D,)),
    }


def reference(x_prompt, x_sample, cache_k, cache_v, state_gla, page_table, p_prompt, p_sample,
              norm_mix, w_in, w_alpha2, b_alpha, gla_norm, w_proj_a,
              lambda_q1, lambda_k1, lambda_q2, lambda_k2, diff_norm, w_proj_b, w_out,
              norm_ffn, w_router_group, b_router_group, w_router_expert, b_router_expert,
              w_exp_gate, w_exp_up, w_exp_down, norm_ple, w_ple_gate, w_ple_proj, norm_final):
    n_prompt = x_prompt.shape[1]
    n_new = x_sample.shape[1]
    past_len = page_table.shape[1] * cache_k.shape[2]
    pos_prompt = jnp.arange(n_prompt, dtype=jnp.int32)
    pos_sample = past_len + jnp.arange(n_new, dtype=jnp.int32)

    hp, hs = x_prompt, x_sample
    sp_list, kp_list, vp_list, ss_list, ks_list, vs_list = [], [], [], [], [], []
    for i in range(DEPTH):
        lam_init = 0.8 - 0.6 * math.exp(-0.3 * i)
        lw = (norm_mix[i], w_in[i], w_alpha2[i], b_alpha[i], gla_norm[i], w_proj_a[i],
              lambda_q1[i], lambda_k1[i], lambda_q2[i], lambda_k2[i], diff_norm[i], w_proj_b[i], w_out[i],
              norm_ffn[i], w_router_group[i], b_router_group[i], w_router_expert[i], b_router_expert[i],
              w_exp_gate[i], w_exp_up[i], w_exp_down[i], norm_ple[i], w_ple_gate[i], w_ple_proj[i])
        s0_prompt = jnp.zeros((x_prompt.shape[0], H_A, DK_A, DV_A), dtype=jnp.float32)
        hp, sp, kp, vp = decoder_layer(hp, p_prompt[i], pos_prompt, s0_prompt, GLA_CHUNK,
                                       diff_attn_prompt, lam_init, *lw)
        attend_sample = functools.partial(diff_attn_paged, cache_k_l=cache_k[i], cache_v_l=cache_v[i],
                                          page_table=page_table)
        hs, ss, ksn, vsn = decoder_layer(hs, p_sample[i], pos_sample, state_gla[i], n_new,
                                         attend_sample, lam_init, *lw)
        sp_list.append(sp); kp_list.append(kp); vp_list.append(vp)
        ss_list.append(ss); ks_list.append(ksn); vs_list.append(vsn)

    y_prompt = rmsnorm(hp, norm_final)
    y_sample = rmsnorm(hs, norm_final)
    return (y_prompt, y_sample,
            jnp.stack(sp_list), jnp.stack(kp_list), jnp.stack(vp_list),
            jnp.stack(ss_list), jnp.stack(ks_list), jnp.stack(vs_list))
```

```python
import functools

import jax
import jax.numpy as jnp
import numpy as np
from jax import lax
from jax.experimental import pallas as pl
from jax.experimental.pallas import tpu as pltpu

F32 = jnp.float32
BF16 = jnp.bfloat16

H_A, DK_A, DV_A = 4, 128, 256
GATE_RANK = 16
GATE_TAU = 16.0
GLA_CHUNK = 64
H_B, DK_B, DV_B = 8, 128, 256
ROPE_THETA = 10000.0
N_GROUPS, EXPERTS_PER_GROUP, D_EXPERT = 4, 8, 256
N_EXPERTS = N_GROUPS * EXPERTS_PER_GROUP
EPS = 1e-6
LAM_INIT = 0.8 - 0.6

LANES = 128
V7X_VMEM_LIMIT = 56 * 1024 * 1024

HB_W = 2 * DK_B
FLASH_TQ, FLASH_TK = 1024, 512
PAGES_PER_STEP = 8
EXPERT_LANE0 = 32
NEG_INF = float("-inf")


def _cparams(*sem):
    return pltpu.CompilerParams(dimension_semantics=sem, vmem_limit_bytes=V7X_VMEM_LIMIT)


def _tile(n, candidates):
    for c in candidates:
        if n % c == 0:
            return c
    return n


def _sigmoid(x):
    return 1.0 / (1.0 + jnp.exp(-x))


def _rms(x, g):
    ms = jnp.mean(x * x, axis=-1, keepdims=True)
    return x * lax.rsqrt(ms + EPS) * g


def _rmsnorm_kernel(x_ref, g_ref, o_ref):
    o_ref[...] = _rms(x_ref[...], g_ref[...]).astype(o_ref.dtype)


def rmsnorm_cast(x, g, out_dtype=BF16):
    m, d = x.shape
    tm = _tile(m, (512, 256, 128, 8))
    return pl.pallas_call(
        _rmsnorm_kernel,
        out_shape=jax.ShapeDtypeStruct((m, d), out_dtype),
        grid=(m // tm,),
        in_specs=[pl.BlockSpec((tm, d), lambda i: (i, 0)), pl.BlockSpec((1, d), lambda i: (0, 0))],
        out_specs=pl.BlockSpec((tm, d), lambda i: (i, 0)),
        compiler_params=_cparams("parallel"),
        name="rmsnorm_cast",
    )(x, g.reshape(1, d))


def _mm_kernel(x_ref, w_ref, *o_refs, act):
    acc = jnp.dot(x_ref[...], w_ref[...], preferred_element_type=F32)
    if act == "sigmoid":
        acc = _sigmoid(acc)
    for o_ref in o_refs:
        o_ref[...] = acc.astype(o_ref.dtype)


def matmul(x, w, out_dtypes, act=None, tn_candidates=(512, 640, 256, 128), name="matmul"):
    m, k = x.shape
    n = w.shape[1]
    tm = _tile(m, (1024, 512, 256, 128))
    tn = _tile(n, tn_candidates)
    outs = pl.pallas_call(
        functools.partial(_mm_kernel, act=act),
        out_shape=[jax.ShapeDtypeStruct((m, n), dt) for dt in out_dtypes],
        grid=(m // tm, n // tn),
        in_specs=[pl.BlockSpec((tm, k), lambda i, j: (i, 0)), pl.BlockSpec((k, tn), lambda i, j: (0, j))],
        out_specs=[pl.BlockSpec((tm, tn), lambda i, j: (i, j)) for _ in out_dtypes],
        compiler_params=_cparams("parallel", "parallel"),
        name=name,
    )(x, w)
    return outs


def _mm_rope_kernel(x_ref, w_ref, cos_ref, sin_ref, *o_refs, scale):
    acc = jnp.dot(x_ref[...], w_ref[...], preferred_element_type=F32)
    cos = cos_ref[...]
    sin = sin_ref[...]
    half = DK_B // 2
    parts = []
    for c in range(acc.shape[1] // DK_B):
        a = acc[:, c * DK_B:(c + 1) * DK_B]
        parts.append(a * cos + pltpu.roll(a, half, axis=1) * sin)
    rot = jnp.concatenate(parts, axis=1)
    if scale != 1.0:
        rot = rot * scale
    for o_ref in o_refs:
        o_ref[...] = rot.astype(o_ref.dtype)


def matmul_rope(x, w, cos, sin, out_dtypes, scale=1.0, name="matmul_rope"):
    m, k = x.shape
    n = w.shape[1]
    tm = _tile(m, (1024, 512, 256, 128))
    tn = _tile(n, (512, 256, 128))
    return pl.pallas_call(
        functools.partial(_mm_rope_kernel, scale=scale),
        out_shape=[jax.ShapeDtypeStruct((m, n), dt) for dt in out_dtypes],
        grid=(m // tm, n // tn),
        in_specs=[
            pl.BlockSpec((tm, k), lambda i, j: (i, 0)),
            pl.BlockSpec((k, tn), lambda i, j: (0, j)),
            pl.BlockSpec((tm, DK_B), lambda i, j: (i, 0)),
            pl.BlockSpec((tm, DK_B), lambda i, j: (i, 0)),
        ],
        out_specs=[pl.BlockSpec((tm, tn), lambda i, j: (i, j)) for _ in out_dtypes],
        compiler_params=_cparams("parallel", "parallel"),
        name=name,
    )(x, w, cos, sin)


def _mix_kernel(oa_ref, ob_ref, wa_ref, wb_ref, sa_ref, sb_ref, o_ref):
    ya = jnp.dot(oa_ref[...], wa_ref[...], preferred_element_type=F32)
    yb = jnp.dot(ob_ref[...], wb_ref[...], preferred_element_type=F32)
    mix = sa_ref[...].astype(F32) * ya + sb_ref[...].astype(F32) * yb
    o_ref[...] = mix.astype(o_ref.dtype)


def branch_mix(oa, ob, w_pa, w_pb, sig_gm):
    m, ka = oa.shape
    kb = ob.shape[1]
    d = w_pa.shape[1]
    tm = _tile(m, (1024, 512, 256, 128))
    tn = _tile(d, (512, 256, 128))
    nb = d // tn
    return pl.pallas_call(
        _mix_kernel,
        out_shape=jax.ShapeDtypeStruct((m, d), BF16),
        grid=(m // tm, nb),
        in_specs=[
            pl.BlockSpec((tm, ka), lambda i, j: (i, 0)),
            pl.BlockSpec((tm, kb), lambda i, j: (i, 0)),
            pl.BlockSpec((ka, tn), lambda i, j: (0, j)),
            pl.BlockSpec((kb, tn), lambda i, j: (0, j)),
            pl.BlockSpec((tm, tn), lambda i, j: (i, j)),
            pl.BlockSpec((tm, tn), lambda i, j: (i, j + nb)),
        ],
        out_specs=pl.BlockSpec((tm, tn), lambda i, j: (i, j)),
        compiler_params=_cparams("parallel", "parallel"),
        name="branch_mix",
    )(oa, ob, w_pa, w_pb, sig_gm, sig_gm)


def _mm_res_kernel(x_ref, w_ref, r_ref, o_ref):
    o_ref[...] = r_ref[...] + jnp.dot(x_ref[...], w_ref[...], preferred_element_type=F32)


def matmul_residual(x, w, res):
    m, k = x.shape
    n = w.shape[1]
    tm = _tile(m, (1024, 512, 256, 128))
    tn = _tile(n, (512, 256, 128))
    return pl.pallas_call(
        _mm_res_kernel,
        out_shape=jax.ShapeDtypeStruct((m, n), F32),
        grid=(m // tm, n // tn),
        in_specs=[
            pl.BlockSpec((tm, k), lambda i, j: (i, 0)),
            pl.BlockSpec((k, tn), lambda i, j: (0, j)),
            pl.BlockSpec((tm, tn), lambda i, j: (i, j)),
        ],
        out_specs=pl.BlockSpec((tm, tn), lambda i, j: (i, j)),
        compiler_params=_cparams("parallel", "parallel"),
        name="out_proj_residual",
    )(x, w, res)


def _gla_kernel(q_ref, k_ref, v_ref, r_ref, ga_ref, wal_ref, bal_ref, gn_ref, *rest, has_s0, cp):
    if has_s0:
        s0_ref, o_ref, s_out_ref, st_ref = rest
    else:
        o_ref, s_out_ref, st_ref = rest
    ci = pl.program_id(2)
    c = q_ref.shape[1]

    @pl.when(ci == 0)
    def _init():
        if has_s0:
            st_ref[...] = s0_ref[0, 0].T
        else:
            st_ref[...] = jnp.zeros_like(st_ref)

    def padded(x):
        x = x.astype(F32)
        if cp == c:
            return x
        return jnp.concatenate([x, jnp.zeros((cp - c, x.shape[1]), F32)], axis=0)

    q = padded(q_ref[0]) * (DK_A ** -0.5)
    k = padded(k_ref[0])
    v = padded(v_ref[0]).astype(BF16)
    ga = padded(ga_ref[0]).astype(BF16)

    x = jnp.dot(ga, wal_ref[0], preferred_element_type=F32) + bal_ref[0]
    logf = (jnp.minimum(x, 0.0) - jnp.log(1.0 + jnp.exp(-jnp.abs(x)))) * (1.0 / GATE_TAU)
    row = lax.broadcasted_iota(jnp.int32, (cp, cp), 0)
    col = lax.broadcasted_iota(jnp.int32, (cp, cp), 1)
    if cp != c:
        logf = jnp.where(lax.broadcasted_iota(jnp.int32, logf.shape, 0) < c, logf, 0.0)
    causal = row >= col
    b = jnp.dot(causal.astype(F32), logf, precision=lax.Precision.HIGHEST, preferred_element_type=F32)
    b_last = b[cp - 1:cp, :]

    qd = (q * jnp.exp(b)).astype(BF16)
    kd = (k * jnp.exp(-b)).astype(BF16)
    ke = (k * jnp.exp(b_last - b)).astype(BF16)

    a = lax.dot_general(qd, kd, (((1,), (1,)), ((), ())), preferred_element_type=F32)
    a = jnp.where(causal, a, 0.0)
    st = st_ref[...]
    o = jnp.dot(a.astype(BF16), v, preferred_element_type=F32)
    o = o + lax.dot_general(qd, st.astype(BF16), (((1,), (1,)), ((), ())), preferred_element_type=F32)
    kv_t = lax.dot_general(v, ke, (((0,), (0,)), ((), ())), preferred_element_type=F32)
    st_new = st * jnp.exp(b_last) + kv_t
    st_ref[...] = st_new

    @pl.when(ci == pl.num_programs(2) - 1)
    def _final():
        s_out_ref[0, 0] = st_new.T

    o = o[:c]
    r = r_ref[0].astype(F32)
    o_ref[0] = (_rms(o, gn_ref[...]) * (r * _sigmoid(r))).astype(o_ref.dtype)


def gla(za, w_alpha2, b_alpha, gla_norm, s0, batch, chunk):
    m = za.shape[0]
    t = m // batch
    n_chunks = t // chunk
    cp = chunk if chunk % 16 == 0 else LANES
    za3 = za.reshape(batch, t, za.shape[1])
    wal = w_alpha2.reshape(GATE_RANK, H_A, DK_A).transpose(1, 0, 2)
    wal = jnp.pad(wal, ((0, 0), (0, LANES - GATE_RANK), (0, 0))).astype(BF16)
    bal = b_alpha.reshape(H_A, 1, DK_A)
    kcol = H_A
    vcol = 2 * H_A * DK_A // DV_A
    rcol = vcol + H_A
    gcol = (2 * H_A * DK_A + 2 * H_A * DV_A) // LANES
    in_specs = [
        pl.BlockSpec((1, chunk, DK_A), lambda b, h, c: (b, c, h)),
        pl.BlockSpec((1, chunk, DK_A), lambda b, h, c: (b, c, kcol + h)),
        pl.BlockSpec((1, chunk, DV_A), lambda b, h, c: (b, c, vcol + h)),
        pl.BlockSpec((1, chunk, DV_A), lambda b, h, c: (b, c, rcol + h)),
        pl.BlockSpec((1, chunk, LANES), lambda b, h, c: (b, c, gcol)),
        pl.BlockSpec((1, LANES, DK_A), lambda b, h, c: (h, 0, 0)),
        pl.BlockSpec((1, 1, DK_A), lambda b, h, c: (h, 0, 0)),
        pl.BlockSpec((1, DV_A), lambda b, h, c: (0, 0)),
    ]
    args = [za3, za3, za3, za3, za3, wal, bal, gla_norm.reshape(1, DV_A)]
    if s0 is not None:
        in_specs.append(pl.BlockSpec((1, 1, DK_A, DV_A), lambda b, h, c: (b, h, 0, 0)))
        args.append(s0)
    oa, s_new = pl.pallas_call(
        functools.partial(_gla_kernel, has_s0=s0 is not None, cp=cp),
        out_shape=[
            jax.ShapeDtypeStruct((batch, t, H_A * DV_A), BF16 if cp == chunk else F32),
            jax.ShapeDtypeStruct((batch, H_A, DK_A, DV_A), F32),
        ],
        grid=(batch, H_A, n_chunks),
        in_specs=in_specs,
        out_specs=[
            pl.BlockSpec((1, chunk, DV_A), lambda b, h, c: (b, c, h)),
            pl.BlockSpec((1, 1, DK_A, DV_A), lambda b, h, c: (b, h, 0, 0)),
        ],
        scratch_shapes=[pltpu.VMEM((DV_A, DK_A), F32)],
        compiler_params=_cparams("parallel", "parallel", "arbitrary"),
        name="gla",
    )(*args)
    return oa.reshape(m, H_A * DV_A).astype(BF16), s_new


def _lambda(lam_ref):
    lv = lam_ref[...]
    s1 = jnp.sum(lv[0:1] * lv[1:2], axis=-1, keepdims=True)
    s2 = jnp.sum(lv[2:3] * lv[3:4], axis=-1, keepdims=True)
    return jnp.exp(s1) - jnp.exp(s2) + LAM_INIT


def _online_softmax_step(s, v, m_ref, l_ref, acc_ref):
    m_prev = m_ref[...]
    m_new = jnp.maximum(m_prev, jnp.max(s, axis=-1, keepdims=True))
    alpha = jnp.exp(m_prev - m_new)
    p = jnp.exp(s - m_new)
    l_ref[...] = alpha * l_ref[...] + jnp.sum(p, axis=-1, keepdims=True)
    acc_ref[...] = alpha * acc_ref[...] + jnp.dot(p.astype(BF16), v, preferred_element_type=F32)
    m_ref[...] = m_new


def _flash_kernel(qi_ref, kj_ref, q_ref, k_ref, v_ref, lam_ref, dn_ref, o_ref,
                  m1, l1, acc1, m2, l2, acc2, *, tq, tk):
    step = pl.program_id(2)
    qi = qi_ref[step]
    kj = kj_ref[step]

    @pl.when(kj == 0)
    def _init():
        for m_ref, l_ref, acc_ref in ((m1, l1, acc1), (m2, l2, acc2)):
            m_ref[...] = jnp.full_like(m_ref, NEG_INF)
            l_ref[...] = jnp.zeros_like(l_ref)
            acc_ref[...] = jnp.zeros_like(acc_ref)

    q = q_ref[0]
    k = k_ref[0]
    v = v_ref[0]
    q_pos = qi * tq + lax.broadcasted_iota(jnp.int32, (tq, tk), 0)
    k_pos = kj * tk + lax.broadcasted_iota(jnp.int32, (tq, tk), 1)
    visible = k_pos <= q_pos
    for mi, (m_ref, l_ref, acc_ref) in enumerate(((m1, l1, acc1), (m2, l2, acc2))):
        s = lax.dot_general(q[:, mi * DK_B:(mi + 1) * DK_B], k[:, mi * DK_B:(mi + 1) * DK_B],
                            (((1,), (1,)), ((), ())), preferred_element_type=F32)
        s = jnp.where(visible, s, NEG_INF)
        _online_softmax_step(s, v, m_ref, l_ref, acc_ref)

    last_kj = ((qi + 1) * tq - 1) // tk

    @pl.when(kj == last_kj)
    def _final():
        lam = _lambda(lam_ref)
        o = acc1[...] / l1[...] - lam * (acc2[...] / l2[...])
        o_ref[0] = (_rms(o, dn_ref[...]) * (1.0 - LAM_INIT)).astype(o_ref.dtype)


def flash_diff_attention(q, k, v, lam_vecs, diff_norm, batch):
    m = q.shape[0]
    t = m // batch
    tq = min(FLASH_TQ, t)
    tk = min(FLASH_TK, t)
    pairs = [(i, j) for i in range(t // tq) for j in range(((i + 1) * tq - 1) // tk + 1)]
    qi = jnp.asarray(np.array([p[0] for p in pairs], np.int32))
    kj = jnp.asarray(np.array([p[1] for p in pairs], np.int32))
    q3 = q.reshape(batch, t, H_B * HB_W)
    k3 = k.reshape(batch, t, H_B * HB_W)
    v3 = v.reshape(batch, t, H_B * DV_B)
    out = pl.pallas_call(
        functools.partial(_flash_kernel, tq=tq, tk=tk),
        out_shape=jax.ShapeDtypeStruct((batch, t, H_B * DV_B), BF16),
        grid_spec=pltpu.PrefetchScalarGridSpec(
            num_scalar_prefetch=2,
            grid=(batch, H_B, len(pairs)),
            in_specs=[
                pl.BlockSpec((1, tq, HB_W), lambda b, h, s, qi, kj: (b, qi[s], h)),
                pl.BlockSpec((1, tk, HB_W), lambda b, h, s, qi, kj: (b, kj[s], h)),
                pl.BlockSpec((1, tk, DV_B), lambda b, h, s, qi, kj: (b, kj[s], h)),
                pl.BlockSpec((4, DK_B), lambda b, h, s, qi, kj: (0, 0)),
                pl.BlockSpec((1, DV_B), lambda b, h, s, qi, kj: (0, 0)),
            ],
            out_specs=pl.BlockSpec((1, tq, DV_B), lambda b, h, s, qi, kj: (b, qi[s], h)),
            scratch_shapes=[
                pltpu.VMEM((tq, 1), F32), pltpu.VMEM((tq, 1), F32), pltpu.VMEM((tq, DV_B), F32),
                pltpu.VMEM((tq, 1), F32), pltpu.VMEM((tq, 1), F32), pltpu.VMEM((tq, DV_B), F32),
            ],
        ),
        compiler_params=_cparams("parallel", "parallel", "arbitrary"),
        name="flash_diff_attention",
    )(qi, kj, q3, k3, v3, lam_vecs, diff_norm.reshape(1, DV_B))
    return out.reshape(m, H_B * DV_B)


def _paged_kernel(pt_ref, q_ref, kn_ref, vn_ref, lam_ref, dn_ref, *rest, pps, page, t_new):
    k_refs = rest[:pps]
    v_refs = rest[pps:2 * pps]
    o_ref = rest[2 * pps]
    qbd, kb, vb, m_ref, l_ref, acc_ref = rest[2 * pps + 1:]
    j = pl.program_id(1)
    n_rows = H_B * 2 * t_new
    rows_per_head = 2 * t_new

    @pl.when(j == 0)
    def _init():
        qt = jnp.concatenate([q_ref[0]] * (n_rows // t_new), axis=0)
        r = lax.broadcasted_iota(jnp.int32, qt.shape, 0) // t_new
        c = lax.broadcasted_iota(jnp.int32, qt.shape, 1) // DK_B
        qbd[...] = jnp.where(r == c, qt, 0.0).astype(BF16)
        m_ref[...] = jnp.full_like(m_ref, NEG_INF)
        l_ref[...] = jnp.zeros_like(l_ref)
        acc_ref[...] = jnp.zeros_like(acc_ref)

    def attend(n_keys, mask):
        s = lax.dot_general(qbd[...], kb[0:n_keys, :], (((1,), (1,)), ((), ())), preferred_element_type=F32)
        if mask is not None:
            s = jnp.where(mask, s, NEG_INF)
        m_prev = m_ref[...]
        m_new = jnp.maximum(m_prev, jnp.max(s, axis=-1, keepdims=True))
        alpha = jnp.exp(m_prev - m_new)
        p = jnp.exp(s - m_new)
        l_ref[...] = alpha * l_ref[...] + jnp.sum(p, axis=-1, keepdims=True)
        p = p.astype(BF16)
        pv = [jnp.dot(p[h * rows_per_head:(h + 1) * rows_per_head, :],
                      vb[0:n_keys, h * DV_B:(h + 1) * DV_B], preferred_element_type=F32)
              for h in range(H_B)]
        acc_ref[...] = alpha * acc_ref[...] + jnp.concatenate(pv, axis=0)
        m_ref[...] = m_new

    for p in range(pps):
        kb[p * page:(p + 1) * page, :] = k_refs[p][0].astype(BF16)
        vb[p * page:(p + 1) * page, :] = v_refs[p][0].astype(BF16)
    attend(pps * page, None)

    @pl.when(j == pl.num_programs(1) - 1)
    def _final():
        kb[0:page, :] = jnp.concatenate(
            [kn_ref[0], jnp.zeros((page - t_new, kn_ref.shape[2]), F32)], axis=0).astype(BF16)
        vb[0:page, :] = jnp.concatenate(
            [vn_ref[0], jnp.zeros((page - t_new, vn_ref.shape[2]), F32)], axis=0).astype(BF16)
        qrow = lax.broadcasted_iota(jnp.int32, (n_rows, page), 0) % t_new
        kcol = lax.broadcasted_iota(jnp.int32, (n_rows, page), 1)
        attend(page, kcol <= qrow)
        lam = _lambda(lam_ref)
        w = acc_ref[...] / l_ref[...]
        heads = []
        for h in range(H_B):
            w1 = w[h * rows_per_head:h * rows_per_head + t_new]
            w2 = w[h * rows_per_head + t_new:(h + 1) * rows_per_head]
            heads.append(_rms(w1 - lam * w2, dn_ref[...]) * (1.0 - LAM_INIT))
        o_ref[0] = jnp.concatenate(heads, axis=1).astype(o_ref.dtype)


def paged_diff_attention(q, k_new, v_new, cache_k, cache_v, page_table, lam_vecs, diff_norm):
    db, t_new, width = q.shape
    page = cache_k.shape[1]
    n_pages = page_table.shape[1]
    pps = min(PAGES_PER_STEP, n_pages)
    assert n_pages % pps == 0 and page >= t_new
    n_rows = H_B * 2 * t_new

    def page_spec(p):
        return pl.BlockSpec((1, page, width), lambda b, j, pt: (pt[b, j * pps + p], 0, 0))

    seq_spec = pl.BlockSpec((1, t_new, width), lambda b, j, pt: (b, 0, 0))
    return pl.pallas_call(
        functools.partial(_paged_kernel, pps=pps, page=page, t_new=t_new),
        out_shape=jax.ShapeDtypeStruct((db, t_new, H_B * DV_B), F32),
        grid_spec=pltpu.PrefetchScalarGridSpec(
            num_scalar_prefetch=1,
            grid=(db, n_pages // pps),
            in_specs=[seq_spec, seq_spec, seq_spec,
                      pl.BlockSpec((4, DK_B), lambda b, j, pt: (0, 0)),
                      pl.BlockSpec((1, DV_B), lambda b, j, pt: (0, 0))]
            + [page_spec(p) for p in range(pps)] + [page_spec(p) for p in range(pps)],
            out_specs=pl.BlockSpec((1, t_new, H_B * DV_B), lambda b, j, pt: (b, 0, 0)),
            scratch_shapes=[
                pltpu.VMEM((n_rows, width), BF16),
                pltpu.VMEM((pps * page, width), BF16),
                pltpu.VMEM((pps * page, width), BF16),
                pltpu.VMEM((n_rows, 1), F32), pltpu.VMEM((n_rows, 1), F32),
                pltpu.VMEM((n_rows, DV_B), F32),
            ],
        ),
        compiler_params=_cparams("parallel", "arbitrary"),
        name="paged_diff_attention",
    )(page_table, q, k_new, v_new, lam_vecs, diff_norm.reshape(1, DV_B),
      *([cache_k] * pps), *([cache_v] * pps))


def _split_bf16(x):
    hi = x.astype(BF16)
    lo = (x - hi.astype(F32)).astype(BF16)
    return hi, lo


def _route(logits):
    lane = lax.broadcasted_iota(jnp.int32, logits.shape, 1)
    lane_f = lane.astype(F32)
    big = float(LANES)
    lg = jnp.where(lane < N_GROUPS, logits, NEG_INF)
    mg = jnp.max(lg, axis=-1, keepdims=True)
    g_idx = jnp.min(jnp.where(lg == mg, lane_f, big), axis=-1, keepdims=True)
    pg_top = 1.0 / jnp.sum(jnp.exp(lg - mg), axis=-1, keepdims=True)
    lo = EXPERT_LANE0 + g_idx * EXPERTS_PER_GROUP
    sel = (lane_f >= lo) & (lane_f < lo + EXPERTS_PER_GROUP)
    le = jnp.where(sel, logits, NEG_INF)
    m1 = jnp.max(le, axis=-1, keepdims=True)
    i1 = jnp.min(jnp.where(le == m1, lane_f, big), axis=-1, keepdims=True)
    le2 = jnp.where(lane_f == i1, NEG_INF, le)
    m2 = jnp.max(le2, axis=-1, keepdims=True)
    i2 = jnp.min(jnp.where(le2 == m2, lane_f, big), axis=-1, keepdims=True)
    e2 = jnp.exp(m2 - m1)
    w1 = pg_top / (1.0 + e2)
    w2 = pg_top * e2 / (1.0 + e2)
    return jnp.where(lane_f == i1, w1, 0.0) + jnp.where(lane_f == i2, w2, 0.0)


def _moe_kernel(h_ref, g_ref, wr_ref, br_ref, wg_ref, wu_ref, wd_ref, o_ref, x_ref, gate_ref):
    e = pl.program_id(1)

    @pl.when(e == 0)
    def _init():
        h = h_ref[...]
        x = _rms(h, g_ref[...])
        x_ref[...] = x.astype(BF16)
        x_hi, x_lo = _split_bf16(x)
        w_hi, w_lo = _split_bf16(wr_ref[...])
        logits = (jnp.dot(x_hi, w_hi, preferred_element_type=F32)
                  + jnp.dot(x_lo, w_hi, preferred_element_type=F32)
                  + jnp.dot(x_hi, w_lo, preferred_element_type=F32)) + br_ref[...]
        gate_ref[...] = _route(logits)
        o_ref[...] = h

    x = x_ref[...]
    hg = jnp.dot(x, wg_ref[0], preferred_element_type=F32)
    hu = jnp.dot(x, wu_ref[0], preferred_element_type=F32)
    lane = lax.broadcasted_iota(jnp.int32, gate_ref.shape, 1)
    gcol = jnp.sum(jnp.where(lane == e + EXPERT_LANE0, gate_ref[...], 0.0), axis=-1, keepdims=True)
    act = (hg * _sigmoid(hg)) * hu * gcol
    o_ref[...] += jnp.dot(act.astype(BF16), wd_ref[0], preferred_element_type=F32)


def moe_dense(h, norm_ffn, w_router, b_router, w_gate, w_up, w_down):
    m, d = h.shape
    tm = _tile(m, (1024, 512, 256, 128))
    return pl.pallas_call(
        _moe_kernel,
        out_shape=jax.ShapeDtypeStruct((m, d), F32),
        grid=(m // tm, N_EXPERTS),
        in_specs=[
            pl.BlockSpec((tm, d), lambda i, e: (i, 0)),
            pl.BlockSpec((1, d), lambda i, e: (0, 0)),
            pl.BlockSpec((d, LANES), lambda i, e: (0, 0)),
            pl.BlockSpec((1, LANES), lambda i, e: (0, 0)),
            pl.BlockSpec((1, d, D_EXPERT), lambda i, e: (e, 0, 0)),
            pl.BlockSpec((1, d, D_EXPERT), lambda i, e: (e, 0, 0)),
            pl.BlockSpec((1, D_EXPERT, d), lambda i, e: (e, 0, 0)),
        ],
        out_specs=pl.BlockSpec((tm, d), lambda i, e: (i, 0)),
        scratch_shapes=[pltpu.VMEM((tm, d), BF16), pltpu.VMEM((tm, LANES), F32)],
        compiler_params=_cparams("parallel", "arbitrary"),
        name="moe_dense",
    )(h, norm_ffn.reshape(1, d), w_router, b_router, w_gate, w_up, w_down)


def _ple_kernel(h_ref, p_ref, gp_ref, gf_ref, wg_ref, wp_ref, o_ref):
    h = h_ref[...]
    xn = _rms(h, gp_ref[...]).astype(BF16)
    gate = _sigmoid(jnp.dot(xn, wg_ref[...], preferred_element_type=F32))
    pp = jnp.dot(p_ref[...].astype(BF16), wp_ref[...], preferred_element_type=F32)
    o_ref[...] = _rms(h + pp * gate, gf_ref[...])


def ple_final(h, p, norm_ple, norm_final, w_gate, w_proj):
    m, d = h.shape
    pd = p.shape[1]
    tm = _tile(m, (512, 256, 128))
    return pl.pallas_call(
        _ple_kernel,
        out_shape=jax.ShapeDtypeStruct((m, d), F32),
        grid=(m // tm,),
        in_specs=[
            pl.BlockSpec((tm, d), lambda i: (i, 0)),
            pl.BlockSpec((tm, pd), lambda i: (i, 0)),
            pl.BlockSpec((1, d), lambda i: (0, 0)),
            pl.BlockSpec((1, d), lambda i: (0, 0)),
            pl.BlockSpec((d, d), lambda i: (0, 0)),
            pl.BlockSpec((pd, d), lambda i: (0, 0)),
        ],
        out_specs=pl.BlockSpec((tm, d), lambda i: (i, 0)),
        compiler_params=_cparams("parallel"),
        name="ple_final",
    )(h, p, norm_ple.reshape(1, d), norm_final.reshape(1, d), w_gate, w_proj)


def _rope_tables(pos):
    half = DK_B // 2
    inv = ROPE_THETA ** (-jnp.arange(half, dtype=F32) / half)
    ang = pos.astype(F32)[:, None] * inv[None, :]
    cos = jnp.cos(ang)
    sin = jnp.sin(ang)
    return jnp.concatenate([cos, cos], axis=-1), jnp.concatenate([-sin, sin], axis=-1)


def _prepare_weights(w_in, w_alpha2, b_alpha, w_proj_a, w_proj_b, w_out, w_router_group, b_router_group,
                     w_router_expert, b_router_expert, w_exp_gate, w_exp_up, w_exp_down, w_ple_gate, w_ple_proj):
    d = w_in.shape[0]
    sizes = [H_A * DK_A, H_A * DK_A, H_A * DV_A, H_A * DV_A, GATE_RANK,
             H_B * HB_W, H_B * HB_W, H_B * DV_B, 2 * d]
    off = np.concatenate([[0], np.cumsum(sizes)])
    w_gla = jnp.concatenate(
        [w_in[:, off[0]:off[4]], jnp.pad(w_in[:, off[4]:off[5]], ((0, 0), (0, LANES - GATE_RANK)))], axis=1)
    w_router = jnp.zeros((d, LANES), F32)
    w_router = w_router.at[:, :N_GROUPS].set(w_router_group)
    w_router = w_router.at[:, EXPERT_LANE0:EXPERT_LANE0 + N_EXPERTS].set(
        w_router_expert.transpose(1, 0, 2).reshape(d, N_EXPERTS))
    b_router = jnp.zeros((1, LANES), F32)
    b_router = b_router.at[0, :N_GROUPS].set(b_router_group)
    b_router = b_router.at[0, EXPERT_LANE0:EXPERT_LANE0 + N_EXPERTS].set(b_router_expert.reshape(N_EXPERTS))
    return dict(
        w_gla=w_gla.astype(BF16),
        w_q=w_in[:, off[5]:off[6]].astype(BF16),
        w_k=w_in[:, off[6]:off[7]].astype(BF16),
        w_v=w_in[:, off[7]:off[8]].astype(BF16),
        w_gm=w_in[:, off[8]:off[9]].astype(BF16),
        w_alpha2=w_alpha2, b_alpha=b_alpha,
        w_pa=w_proj_a.astype(BF16), w_pb=w_proj_b.astype(BF16), w_out=w_out.astype(BF16),
        w_router=w_router, b_router=b_router,
        w_eg=w_exp_gate.astype(BF16), w_eu=w_exp_up.astype(BF16), w_ed=w_exp_down.astype(BF16),
        w_pg=w_ple_gate.astype(BF16), w_pp=w_ple_proj.astype(BF16),
    )


def _layer(x, p_emb, pos, batch, chunk, s0, attend, W, norms, lam_vecs, za_dtype):
    norm_mix, gla_norm, diff_norm, norm_ffn, norm_ple, norm_final = norms
    hn = rmsnorm_cast(x, norm_mix)
    cos, sin = _rope_tables(pos)
    (za,) = matmul(hn, W["w_gla"], [za_dtype], name="in_proj_gla")
    (q,) = matmul_rope(hn, W["w_q"], cos, sin, [attend.q_dtype], scale=DK_B ** -0.5, name="in_proj_q")
    k_outs = matmul_rope(hn, W["w_k"], cos, sin, [F32] + attend.kv_extra, name="in_proj_k")
    v_outs = matmul(hn, W["w_v"], [F32] + attend.kv_extra, name="in_proj_v")
    (sig_gm,) = matmul(hn, W["w_gm"], [BF16], act="sigmoid", name="in_proj_gates")

    oa, s_new = gla(za, W["w_alpha2"], W["b_alpha"], gla_norm, s0, batch, chunk)
    ob = attend(q, k_outs, v_outs, lam_vecs, diff_norm)
    mix = branch_mix(oa, ob, W["w_pa"], W["w_pb"], sig_gm)
    h1 = matmul_residual(mix, W["w_out"], x)
    h2 = moe_dense(h1, norm_ffn, W["w_router"], W["b_router"], W["w_eg"], W["w_eu"], W["w_ed"])
    y = ple_final(h2, p_emb, norm_ple, norm_final, W["w_pg"], W["w_pp"])
    return y, s_new, k_outs[0], v_outs[0]


class _PromptAttention:
    q_dtype = BF16
    kv_extra = [BF16]

    def __init__(self, batch):
        self.batch = batch

    def __call__(self, q, k_outs, v_outs, lam_vecs, diff_norm):
        return flash_diff_attention(q, k_outs[1], v_outs[1], lam_vecs, diff_norm, self.batch)


class _PagedAttention:
    q_dtype = F32
    kv_extra = []

    def __init__(self, batch, cache_k, cache_v, page_table):
        self.batch, self.cache_k, self.cache_v, self.page_table = batch, cache_k, cache_v, page_table

    def __call__(self, q, k_outs, v_outs, lam_vecs, diff_norm):
        b = self.batch
        t = q.shape[0] // b
        out = paged_diff_attention(q.reshape(b, t, -1), k_outs[0].reshape(b, t, -1), v_outs[0].reshape(b, t, -1),
                                   self.cache_k, self.cache_v, self.page_table, lam_vecs, diff_norm)
        return out.reshape(b * t, -1).astype(BF16)


def kernel(x_prompt, x_sample, cache_k, cache_v, state_gla, page_table, p_prompt, p_sample, norm_mix, w_in, w_alpha2, b_alpha, gla_norm, w_proj_a, lambda_q1, lambda_k1, lambda_q2, lambda_k2, diff_norm, w_proj_b, w_out, norm_ffn, w_router_group, b_router_group, w_router_expert, b_router_expert, w_exp_gate, w_exp_up, w_exp_down, norm_ple, w_ple_gate, w_ple_proj, norm_final):
    assert w_in.shape[0] == 1, "one layer"
    bp, tp, d = x_prompt.shape
    bs, ts, _ = x_sample.shape
    n_pool, page = cache_k.shape[1], cache_k.shape[2]
    past_len = page_table.shape[1] * page

    W = _prepare_weights(w_in[0], w_alpha2[0], b_alpha[0], w_proj_a[0], w_proj_b[0], w_out[0],
                         w_router_group[0], b_router_group[0], w_router_expert[0], b_router_expert[0],
                         w_exp_gate[0], w_exp_up[0], w_exp_down[0], w_ple_gate[0], w_ple_proj[0])
    norms = (norm_mix[0], gla_norm[0], diff_norm[0], norm_ffn[0], norm_ple[0], norm_final)
    lam_vecs = jnp.stack([lambda_q1[0], lambda_k1[0], lambda_q2[0], lambda_k2[0]])

    pos_p = jnp.tile(jnp.arange(tp, dtype=jnp.int32), bp)
    yp, sp, kp, vp = _layer(x_prompt.reshape(bp * tp, d), p_prompt[0].reshape(bp * tp, -1), pos_p, bp,
                            min(GLA_CHUNK, tp), None, _PromptAttention(bp), W, norms, lam_vecs, BF16)

    pos_s = jnp.tile(past_len + jnp.arange(ts, dtype=jnp.int32), bs)
    attend_s = _PagedAttention(bs, cache_k[0].reshape(n_pool, page, -1), cache_v[0].reshape(n_pool, page, -1),
                               page_table)
    ys, ss, ks, vs = _layer(x_sample.reshape(bs * ts, d), p_sample[0].reshape(bs * ts, -1), pos_s, bs,
                            ts, state_gla[0], attend_s, W, norms, lam_vecs, F32)

    return (yp.reshape(bp, tp, d), ys.reshape(bs, ts, d),
            sp[None], kp.reshape(1, bp, tp, H_B, 2, DK_B), vp.reshape(1, bp, tp, H_B, DV_B),
            ss[None], ks.reshape(1, bs, ts, H_B, 2, DK_B), vs.reshape(1, bs, ts, H_B, DV_B))
```

```python
import functools

import jax
import jax.numpy as jnp
import numpy as np
from jax import lax
from jax.experimental import pallas as pl
from jax.experimental.pallas import tpu as pltpu

F32 = jnp.float32
BF16 = jnp.bfloat16

H_A, DK_A, DV_A = 4, 128, 256
GATE_RANK = 16
GATE_TAU = 16.0
GLA_CHUNK = 64
H_B, DK_B, DV_B = 8, 128, 256
ROPE_THETA = 10000.0
N_GROUPS, EXPERTS_PER_GROUP, D_EXPERT = 4, 8, 256
N_EXPERTS = N_GROUPS * EXPERTS_PER_GROUP
EPS = 1e-6
LAM_INIT = 0.8 - 0.6

LANES = 128
V7X_VMEM_LIMIT = 56 * 1024 * 1024

HB_W = 2 * DK_B
FLASH_TQ, FLASH_TK = 1024, 1024
FLASH_RQ = 128
LOG2_E = 1.4426950408889634
PAGES_PER_STEP = 8
EXPERT_LANE0 = 32
NEG_INF = float("-inf")


def _cparams(*sem):
    return pltpu.CompilerParams(dimension_semantics=sem, vmem_limit_bytes=V7X_VMEM_LIMIT)


def _tile(n, candidates):
    for c in candidates:
        if n % c == 0:
            return c
    return n


def _sigmoid(x):
    return 1.0 / (1.0 + jnp.exp(-x))


def _rms(x, g):
    ms = jnp.mean(x * x, axis=-1, keepdims=True)
    return x * lax.rsqrt(ms + EPS) * g


def _rmsnorm_kernel(x_ref, g_ref, o_ref):
    o_ref[...] = _rms(x_ref[...], g_ref[...]).astype(o_ref.dtype)


def rmsnorm_cast(x, g, out_dtype=BF16):
    m, d = x.shape
    tm = _tile(m, (512, 256, 128, 8))
    return pl.pallas_call(
        _rmsnorm_kernel,
        out_shape=jax.ShapeDtypeStruct((m, d), out_dtype),
        grid=(m // tm,),
        in_specs=[pl.BlockSpec((tm, d), lambda i: (i, 0)), pl.BlockSpec((1, d), lambda i: (0, 0))],
        out_specs=pl.BlockSpec((tm, d), lambda i: (i, 0)),
        compiler_params=_cparams("parallel"),
        name="rmsnorm_cast",
    )(x, g.reshape(1, d))


def _mm_kernel(x_ref, w_ref, *o_refs, act):
    acc = jnp.dot(x_ref[...], w_ref[...], preferred_element_type=F32)
    if act == "sigmoid":
        acc = _sigmoid(acc)
    for o_ref in o_refs:
        o_ref[...] = acc.astype(o_ref.dtype)


def matmul(x, w, out_dtypes, act=None, tn_candidates=(1024, 640, 512, 256, 128), name="matmul"):
    m, k = x.shape
    n = w.shape[1]
    tm = _tile(m, (1024, 512, 256, 128))
    tn = _tile(n, tn_candidates)
    outs = pl.pallas_call(
        functools.partial(_mm_kernel, act=act),
        out_shape=[jax.ShapeDtypeStruct((m, n), dt) for dt in out_dtypes],
        grid=(m // tm, n // tn),
        in_specs=[pl.BlockSpec((tm, k), lambda i, j: (i, 0)), pl.BlockSpec((k, tn), lambda i, j: (0, j))],
        out_specs=[pl.BlockSpec((tm, tn), lambda i, j: (i, j)) for _ in out_dtypes],
        compiler_params=_cparams("parallel", "parallel"),
        name=name,
    )(x, w)
    return outs


def _mm_rope_kernel(x_ref, w_ref, cos_ref, sin_ref, *o_refs, scale):
    acc = jnp.dot(x_ref[...], w_ref[...], preferred_element_type=F32)
    cos = cos_ref[...]
    sin = sin_ref[...]
    half = DK_B // 2
    parts = []
    for c in range(acc.shape[1] // DK_B):
        a = acc[:, c * DK_B:(c + 1) * DK_B]
        parts.append(a * cos + pltpu.roll(a, half, axis=1) * sin)
    rot = jnp.concatenate(parts, axis=1)
    if scale != 1.0:
        rot = rot * scale
    for o_ref in o_refs:
        o_ref[...] = rot.astype(o_ref.dtype)


def matmul_rope(x, w, cos, sin, out_dtypes, scale=1.0, name="matmul_rope"):
    m, k = x.shape
    n = w.shape[1]
    tm = _tile(m, (1024, 512, 256, 128))
    tn = _tile(n, (1024, 512, 256, 128))
    return pl.pallas_call(
        functools.partial(_mm_rope_kernel, scale=scale),
        out_shape=[jax.ShapeDtypeStruct((m, n), dt) for dt in out_dtypes],
        grid=(m // tm, n // tn),
        in_specs=[
            pl.BlockSpec((tm, k), lambda i, j: (i, 0)),
            pl.BlockSpec((k, tn), lambda i, j: (0, j)),
            pl.BlockSpec((tm, DK_B), lambda i, j: (i, 0)),
            pl.BlockSpec((tm, DK_B), lambda i, j: (i, 0)),
        ],
        out_specs=[pl.BlockSpec((tm, tn), lambda i, j: (i, j)) for _ in out_dtypes],
        compiler_params=_cparams("parallel", "parallel"),
        name=name,
    )(x, w, cos, sin)


def _mix_kernel(oa_ref, ob_ref, wa_ref, wb_ref, sa_ref, sb_ref, o_ref):
    ya = jnp.dot(oa_ref[...], wa_ref[...], preferred_element_type=F32)
    yb = jnp.dot(ob_ref[...], wb_ref[...], preferred_element_type=F32)
    mix = sa_ref[...].astype(F32) * ya + sb_ref[...].astype(F32) * yb
    o_ref[...] = mix.astype(o_ref.dtype)


def branch_mix(oa, ob, w_pa, w_pb, sig_gm):
    m, ka = oa.shape
    kb = ob.shape[1]
    d = w_pa.shape[1]
    tm = _tile(m, (1024, 512, 256, 128))
    tn = _tile(d, (1024, 512, 256, 128))
    nb = d // tn
    return pl.pallas_call(
        _mix_kernel,
        out_shape=jax.ShapeDtypeStruct((m, d), BF16),
        grid=(m // tm, nb),
        in_specs=[
            pl.BlockSpec((tm, ka), lambda i, j: (i, 0)),
            pl.BlockSpec((tm, kb), lambda i, j: (i, 0)),
            pl.BlockSpec((ka, tn), lambda i, j: (0, j)),
            pl.BlockSpec((kb, tn), lambda i, j: (0, j)),
            pl.BlockSpec((tm, tn), lambda i, j: (i, j)),
            pl.BlockSpec((tm, tn), lambda i, j: (i, j + nb)),
        ],
        out_specs=pl.BlockSpec((tm, tn), lambda i, j: (i, j)),
        compiler_params=_cparams("parallel", "parallel"),
        name="branch_mix",
    )(oa, ob, w_pa, w_pb, sig_gm, sig_gm)


def _mm_res_kernel(x_ref, w_ref, r_ref, o_ref):
    o_ref[...] = r_ref[...] + jnp.dot(x_ref[...], w_ref[...], preferred_element_type=F32)


def matmul_residual(x, w, res):
    m, k = x.shape
    n = w.shape[1]
    tm = _tile(m, (1024, 512, 256, 128))
    tn = _tile(n, (1024, 512, 256, 128))
    return pl.pallas_call(
        _mm_res_kernel,
        out_shape=jax.ShapeDtypeStruct((m, n), F32),
        grid=(m // tm, n // tn),
        in_specs=[
            pl.BlockSpec((tm, k), lambda i, j: (i, 0)),
            pl.BlockSpec((k, tn), lambda i, j: (0, j)),
            pl.BlockSpec((tm, tn), lambda i, j: (i, j)),
        ],
        out_specs=pl.BlockSpec((tm, tn), lambda i, j: (i, j)),
        compiler_params=_cparams("parallel", "parallel"),
        name="out_proj_residual",
    )(x, w, res)


def _gla_kernel(q_ref, k_ref, v_ref, r_ref, ga_ref, wal_ref, bal_ref, gn_ref, *rest, has_s0, cp):
    if has_s0:
        s0_ref, o_ref, s_out_ref, st_ref = rest
    else:
        o_ref, s_out_ref, st_ref = rest
    ci = pl.program_id(1)
    c = q_ref.shape[1]

    @pl.when(ci == 0)
    def _init():
        for h in range(H_A):
            if has_s0:
                st_ref[h] = s0_ref[0, h].T
            else:
                st_ref[h] = jnp.zeros(st_ref.shape[1:], F32)

    def padded(x):
        x = x.astype(F32)
        if cp == c:
            return x
        return jnp.concatenate([x, jnp.zeros((cp - c, x.shape[1]), F32)], axis=0)

    row = lax.broadcasted_iota(jnp.int32, (cp, cp), 0)
    col = lax.broadcasted_iota(jnp.int32, (cp, cp), 1)
    causal = row >= col
    tril = causal.astype(F32)
    ga = padded(ga_ref[0]).astype(BF16)
    x = jnp.dot(ga, wal_ref[...], preferred_element_type=F32) + bal_ref[...]
    logf = (jnp.minimum(x, 0.0) - jnp.log(1.0 + jnp.exp(-jnp.abs(x)))) * (1.0 / GATE_TAU)
    if cp != c:
        logf = jnp.where(lax.broadcasted_iota(jnp.int32, logf.shape, 0) < c, logf, 0.0)
    b_all = jnp.dot(tril, logf, precision=lax.Precision.HIGHEST, preferred_element_type=F32)
    outs = []
    for h in range(H_A):
        q = padded(q_ref[0, :, h * DK_A:(h + 1) * DK_A]) * (DK_A ** -0.5)
        k = padded(k_ref[0, :, h * DK_A:(h + 1) * DK_A])
        v = padded(v_ref[0, :, h * DV_A:(h + 1) * DV_A]).astype(BF16)
        b = b_all[:, h * DK_A:(h + 1) * DK_A]
        b_last = b[cp - 1:cp, :]

        qd = (q * jnp.exp(b)).astype(BF16)
        kd = (k * jnp.exp(-b)).astype(BF16)
        ke = (k * jnp.exp(b_last - b)).astype(BF16)

        a = lax.dot_general(qd, kd, (((1,), (1,)), ((), ())), preferred_element_type=F32)
        a = jnp.where(causal, a, 0.0)
        st = st_ref[h]
        o = jnp.dot(a.astype(BF16), v, preferred_element_type=F32)
        o = o + lax.dot_general(qd, st.astype(BF16), (((1,), (1,)), ((), ())), preferred_element_type=F32)
        kv_t = lax.dot_general(v, ke, (((0,), (0,)), ((), ())), preferred_element_type=F32)
        st_ref[h] = st * jnp.exp(b_last) + kv_t

        r = r_ref[0, :, h * DV_A:(h + 1) * DV_A].astype(F32)
        outs.append(_rms(o[:c], gn_ref[...]) * (r * _sigmoid(r)))
    o_ref[0] = jnp.concatenate(outs, axis=1).astype(o_ref.dtype)

    @pl.when(ci == pl.num_programs(1) - 1)
    def _final():
        for h in range(H_A):
            s_out_ref[0, h] = st_ref[h].T


def gla(za, w_alpha2, b_alpha, gla_norm, s0, batch, chunk):
    m = za.shape[0]
    t = m // batch
    n_chunks = t // chunk
    cp = chunk if chunk % 16 == 0 else LANES
    za3 = za.reshape(batch, t, za.shape[1])
    wal = jnp.pad(w_alpha2, ((0, LANES - GATE_RANK), (0, 0))).astype(BF16)
    bal = b_alpha.reshape(1, H_A * DK_A)
    qk_w = H_A * DK_A
    vr_w = H_A * DV_A
    gcol = (2 * qk_w + 2 * vr_w) // LANES
    in_specs = [
        pl.BlockSpec((1, chunk, qk_w), lambda b, c: (b, c, 0)),
        pl.BlockSpec((1, chunk, qk_w), lambda b, c: (b, c, 1)),
        pl.BlockSpec((1, chunk, vr_w), lambda b, c: (b, c, 1)),
        pl.BlockSpec((1, chunk, vr_w), lambda b, c: (b, c, 2)),
        pl.BlockSpec((1, chunk, LANES), lambda b, c: (b, c, gcol)),
        pl.BlockSpec((LANES, qk_w), lambda b, c: (0, 0)),
        pl.BlockSpec((1, qk_w), lambda b, c: (0, 0)),
        pl.BlockSpec((1, DV_A), lambda b, c: (0, 0)),
    ]
    args = [za3, za3, za3, za3, za3, wal, bal, gla_norm.reshape(1, DV_A)]
    if s0 is not None:
        in_specs.append(pl.BlockSpec((1, H_A, DK_A, DV_A), lambda b, c: (b, 0, 0, 0)))
        args.append(s0)
    oa, s_new = pl.pallas_call(
        functools.partial(_gla_kernel, has_s0=s0 is not None, cp=cp),
        out_shape=[
            jax.ShapeDtypeStruct((batch, t, vr_w), BF16 if cp == chunk else F32),
            jax.ShapeDtypeStruct((batch, H_A, DK_A, DV_A), F32),
        ],
        grid=(batch, n_chunks),
        in_specs=in_specs,
        out_specs=[
            pl.BlockSpec((1, chunk, vr_w), lambda b, c: (b, c, 0)),
            pl.BlockSpec((1, H_A, DK_A, DV_A), lambda b, c: (b, 0, 0, 0)),
        ],
        scratch_shapes=[pltpu.VMEM((H_A, DV_A, DK_A), F32)],
        compiler_params=_cparams("parallel", "arbitrary"),
        name="gla",
    )(*args)
    return oa.reshape(m, vr_w).astype(BF16), s_new


def _lambda(lam_ref):
    lv = lam_ref[...]
    s1 = jnp.sum(lv[0:1] * lv[1:2], axis=-1, keepdims=True)
    s2 = jnp.sum(lv[2:3] * lv[3:4], axis=-1, keepdims=True)
    return jnp.exp(s1) - jnp.exp(s2) + LAM_INIT


def _lane_tile(x, width):
    return x if width == LANES else jnp.concatenate([x] * (width // LANES), axis=1)


def _flash_kernel(qi_ref, kj_ref, q_ref, k_ref, v_ref, lam_ref, dn_ref, o_ref,
                  m_ref, l_ref, acc_ref, *, tq, tk, rq):
    step = pl.program_id(2)
    qi = qi_ref[step]
    kj = kj_ref[step]

    @pl.when(kj == 0)
    def _init():
        m_ref[...] = jnp.full_like(m_ref, NEG_INF)
        l_ref[...] = jnp.zeros_like(l_ref)
        acc_ref[...] = jnp.zeros_like(acc_ref)

    def update(masked):
        v = v_ref[0]
        for r0 in range(0, tq, rq):
            if masked:
                q_pos = qi * tq + r0 + lax.broadcasted_iota(jnp.int32, (rq, tk), 0)
                k_pos = kj * tk + lax.broadcasted_iota(jnp.int32, (rq, tk), 1)
                visible = k_pos <= q_pos
            for mi in range(2):
                q = q_ref[0, r0:r0 + rq, mi * DK_B:(mi + 1) * DK_B]
                k = k_ref[0, :, mi * DK_B:(mi + 1) * DK_B]
                s = lax.dot_general(q, k, (((1,), (1,)), ((), ())), preferred_element_type=F32)
                if masked:
                    s = jnp.where(visible, s, NEG_INF)
                m_prev = m_ref[mi, r0:r0 + rq, :]
                m_new = jnp.maximum(m_prev, jnp.max(s, axis=-1, keepdims=True))
                alpha = jnp.exp2(m_prev - m_new)
                p = jnp.exp2(s - _lane_tile(m_new, tk))
                l_ref[mi, r0:r0 + rq, :] = alpha * l_ref[mi, r0:r0 + rq, :] + jnp.sum(p, axis=-1, keepdims=True)
                pv = jnp.dot(p.astype(BF16), v, preferred_element_type=F32)
                acc_ref[mi, r0:r0 + rq, :] = _lane_tile(alpha, DV_B) * acc_ref[mi, r0:r0 + rq, :] + pv
                m_ref[mi, r0:r0 + rq, :] = m_new

    on_diagonal = (kj + 1) * tk - 1 > qi * tq

    @pl.when(on_diagonal)
    def _masked():
        update(True)

    @pl.when(jnp.logical_not(on_diagonal))
    def _unmasked():
        update(False)

    last_kj = ((qi + 1) * tq - 1) // tk

    @pl.when(kj == last_kj)
    def _final():
        lam = _lambda(lam_ref)
        o = (acc_ref[0] / _lane_tile(l_ref[0], DV_B)
             - lam * (acc_ref[1] / _lane_tile(l_ref[1], DV_B)))
        o_ref[0] = (_rms(o, dn_ref[...]) * (1.0 - LAM_INIT)).astype(o_ref.dtype)


def flash_diff_attention(q, k, v, lam_vecs, diff_norm, batch):
    m = q.shape[0]
    t = m // batch
    tq = min(FLASH_TQ, t)
    tk = min(FLASH_TK, t)
    pairs = [(i, j) for i in range(t // tq) for j in range(((i + 1) * tq - 1) // tk + 1)]
    qi = jnp.asarray(np.array([p[0] for p in pairs], np.int32))
    kj = jnp.asarray(np.array([p[1] for p in pairs], np.int32))
    q3 = q.reshape(batch, t, H_B * HB_W)
    k3 = k.reshape(batch, t, H_B * HB_W)
    v3 = v.reshape(batch, t, H_B * DV_B)
    out = pl.pallas_call(
        functools.partial(_flash_kernel, tq=tq, tk=tk, rq=min(FLASH_RQ, tq)),
        out_shape=jax.ShapeDtypeStruct((batch, t, H_B * DV_B), BF16),
        grid_spec=pltpu.PrefetchScalarGridSpec(
            num_scalar_prefetch=2,
            grid=(batch, H_B, len(pairs)),
            in_specs=[
                pl.BlockSpec((1, tq, HB_W), lambda b, h, s, qi, kj: (b, qi[s], h)),
                pl.BlockSpec((1, tk, HB_W), lambda b, h, s, qi, kj: (b, kj[s], h)),
                pl.BlockSpec((1, tk, DV_B), lambda b, h, s, qi, kj: (b, kj[s], h)),
                pl.BlockSpec((4, DK_B), lambda b, h, s, qi, kj: (0, 0)),
                pl.BlockSpec((1, DV_B), lambda b, h, s, qi, kj: (0, 0)),
            ],
            out_specs=pl.BlockSpec((1, tq, DV_B), lambda b, h, s, qi, kj: (b, qi[s], h)),
            scratch_shapes=[
                pltpu.VMEM((2, tq, LANES), F32), pltpu.VMEM((2, tq, LANES), F32),
                pltpu.VMEM((2, tq, DV_B), F32),
            ],
        ),
        compiler_params=_cparams("parallel", "parallel", "arbitrary"),
        name="flash_diff_attention",
    )(qi, kj, q3, k3, v3, lam_vecs, diff_norm.reshape(1, DV_B))
    return out.reshape(m, H_B * DV_B)


def _paged_kernel(pt_ref, q_ref, kn_ref, vn_ref, lam_ref, dn_ref, *rest, pps, page, t_new):
    k_refs = rest[:pps]
    v_refs = rest[pps:2 * pps]
    o_ref = rest[2 * pps]
    qbd, kb, vb, m_ref, l_ref, acc_ref = rest[2 * pps + 1:]
    j = pl.program_id(1)
    n_rows = H_B * 2 * t_new
    rows_per_head = 2 * t_new

    @pl.when(j == 0)
    def _init():
        qt = jnp.concatenate([q_ref[0]] * (n_rows // t_new), axis=0)
        r = lax.broadcasted_iota(jnp.int32, qt.shape, 0) // t_new
        c = lax.broadcasted_iota(jnp.int32, qt.shape, 1) // DK_B
        qbd[...] = jnp.where(r == c, qt, 0.0).astype(BF16)
        m_ref[...] = jnp.full_like(m_ref, NEG_INF)
        l_ref[...] = jnp.zeros_like(l_ref)
        acc_ref[...] = jnp.zeros_like(acc_ref)

    def attend(n_keys, mask):
        s = lax.dot_general(qbd[...], kb[0:n_keys, :], (((1,), (1,)), ((), ())), preferred_element_type=F32)
        if mask is not None:
            s = jnp.where(mask, s, NEG_INF)
        m_prev = m_ref[...]
        m_new = jnp.maximum(m_prev, jnp.max(s, axis=-1, keepdims=True))
        alpha = jnp.exp(m_prev - m_new)
        p = jnp.exp(s - m_new)
        l_ref[...] = alpha * l_ref[...] + jnp.sum(p, axis=-1, keepdims=True)
        p = p.astype(BF16)
        pv = [jnp.dot(p[h * rows_per_head:(h + 1) * rows_per_head, :],
                      vb[0:n_keys, h * DV_B:(h + 1) * DV_B], preferred_element_type=F32)
              for h in range(H_B)]
        acc_ref[...] = alpha * acc_ref[...] + jnp.concatenate(pv, axis=0)
        m_ref[...] = m_new

    for p in range(pps):
        kb[p * page:(p + 1) * page, :] = k_refs[p][0].astype(BF16)
        vb[p * page:(p + 1) * page, :] = v_refs[p][0].astype(BF16)
    attend(pps * page, None)

    @pl.when(j == pl.num_programs(1) - 1)
    def _final():
        kb[0:page, :] = jnp.concatenate(
            [kn_ref[0], jnp.zeros((page - t_new, kn_ref.shape[2]), F32)], axis=0).astype(BF16)
        vb[0:page, :] = jnp.concatenate(
            [vn_ref[0], jnp.zeros((page - t_new, vn_ref.shape[2]), F32)], axis=0).astype(BF16)
        qrow = lax.broadcasted_iota(jnp.int32, (n_rows, page), 0) % t_new
        kcol = lax.broadcasted_iota(jnp.int32, (n_rows, page), 1)
        attend(page, kcol <= qrow)
        lam = _lambda(lam_ref)
        w = acc_ref[...] / l_ref[...]
        heads = []
        for h in range(H_B):
            w1 = w[h * rows_per_head:h * rows_per_head + t_new]
            w2 = w[h * rows_per_head + t_new:(h + 1) * rows_per_head]
            heads.append(_rms(w1 - lam * w2, dn_ref[...]) * (1.0 - LAM_INIT))
        o_ref[0] = jnp.concatenate(heads, axis=1).astype(o_ref.dtype)


def paged_diff_attention(q, k_new, v_new, cache_k, cache_v, page_table, lam_vecs, diff_norm):
    db, t_new, width = q.shape
    page = cache_k.shape[1]
    n_pages = page_table.shape[1]
    pps = min(PAGES_PER_STEP, n_pages)
    assert n_pages % pps == 0 and page >= t_new
    n_rows = H_B * 2 * t_new

    def page_spec(p):
        return pl.BlockSpec((1, page, width), lambda b, j, pt: (pt[b, j * pps + p], 0, 0))

    seq_spec = pl.BlockSpec((1, t_new, width), lambda b, j, pt: (b, 0, 0))
    return pl.pallas_call(
        functools.partial(_paged_kernel, pps=pps, page=page, t_new=t_new),
        out_shape=jax.ShapeDtypeStruct((db, t_new, H_B * DV_B), F32),
        grid_spec=pltpu.PrefetchScalarGridSpec(
            num_scalar_prefetch=1,
            grid=(db, n_pages // pps),
            in_specs=[seq_spec, seq_spec, seq_spec,
                      pl.BlockSpec((4, DK_B), lambda b, j, pt: (0, 0)),
                      pl.BlockSpec((1, DV_B), lambda b, j, pt: (0, 0))]
            + [page_spec(p) for p in range(pps)] + [page_spec(p) for p in range(pps)],
            out_specs=pl.BlockSpec((1, t_new, H_B * DV_B), lambda b, j, pt: (b, 0, 0)),
            scratch_shapes=[
                pltpu.VMEM((n_rows, width), BF16),
                pltpu.VMEM((pps * page, width), BF16),
                pltpu.VMEM((pps * page, width), BF16),
                pltpu.VMEM((n_rows, 1), F32), pltpu.VMEM((n_rows, 1), F32),
                pltpu.VMEM((n_rows, DV_B), F32),
            ],
        ),
        compiler_params=_cparams("parallel", "arbitrary"),
        name="paged_diff_attention",
    )(page_table, q, k_new, v_new, lam_vecs, diff_norm.reshape(1, DV_B),
      *([cache_k] * pps), *([cache_v] * pps))


def _split_bf16(x):
    hi = x.astype(BF16)
    lo = (x - hi.astype(F32)).astype(BF16)
    return hi, lo


def _route(logits):
    lane = lax.broadcasted_iota(jnp.int32, logits.shape, 1)
    lane_f = lane.astype(F32)
    big = float(LANES)
    lg = jnp.where(lane < N_GROUPS, logits, NEG_INF)
    mg = jnp.max(lg, axis=-1, keepdims=True)
    g_idx = jnp.min(jnp.where(lg == mg, lane_f, big), axis=-1, keepdims=True)
    pg_top = 1.0 / jnp.sum(jnp.exp(lg - mg), axis=-1, keepdims=True)
    lo = EXPERT_LANE0 + g_idx * EXPERTS_PER_GROUP
    sel = (lane_f >= lo) & (lane_f < lo + EXPERTS_PER_GROUP)
    le = jnp.where(sel, logits, NEG_INF)
    m1 = jnp.max(le, axis=-1, keepdims=True)
    i1 = jnp.min(jnp.where(le == m1, lane_f, big), axis=-1, keepdims=True)
    le2 = jnp.where(lane_f == i1, NEG_INF, le)
    m2 = jnp.max(le2, axis=-1, keepdims=True)
    i2 = jnp.min(jnp.where(le2 == m2, lane_f, big), axis=-1, keepdims=True)
    e2 = jnp.exp(m2 - m1)
    w1 = pg_top / (1.0 + e2)
    w2 = pg_top * e2 / (1.0 + e2)
    return jnp.where(lane_f == i1, w1, 0.0) + jnp.where(lane_f == i2, w2, 0.0)


def _moe_kernel(h_ref, g_ref, wr_ref, br_ref, wg_ref, wu_ref, wd_ref, o_ref, x_ref, gate_ref):
    e = pl.program_id(1)

    @pl.when(e == 0)
    def _init():
        h = h_ref[...]
        x = _rms(h, g_ref[...])
        x_ref[...] = x.astype(BF16)
        x_hi, x_lo = _split_bf16(x)
        w_hi, w_lo = _split_bf16(wr_ref[...])
        logits = (jnp.dot(x_hi, w_hi, preferred_element_type=F32)
                  + jnp.dot(x_lo, w_hi, preferred_element_type=F32)
                  + jnp.dot(x_hi, w_lo, preferred_element_type=F32)) + br_ref[...]
        gate_ref[...] = _route(logits)
        o_ref[...] = h

    x = x_ref[...]
    hg = jnp.dot(x, wg_ref[0], preferred_element_type=F32)
    hu = jnp.dot(x, wu_ref[0], preferred_element_type=F32)
    lane = lax.broadcasted_iota(jnp.int32, gate_ref.shape, 1)
    gcol = jnp.sum(jnp.where(lane == e + EXPERT_LANE0, gate_ref[...], 0.0), axis=-1, keepdims=True)
    act = (hg * _sigmoid(hg)) * hu * gcol
    o_ref[...] += jnp.dot(act.astype(BF16), wd_ref[0], preferred_element_type=F32)


def moe_dense(h, norm_ffn, w_router, b_router, w_gate, w_up, w_down):
    m, d = h.shape
    tm = _tile(m, (1024, 512, 256, 128))
    return pl.pallas_call(
        _moe_kernel,
        out_shape=jax.ShapeDtypeStruct((m, d), F32),
        grid=(m // tm, N_EXPERTS),
        in_specs=[
            pl.BlockSpec((tm, d), lambda i, e: (i, 0)),
            pl.BlockSpec((1, d), lambda i, e: (0, 0)),
            pl.BlockSpec((d, LANES), lambda i, e: (0, 0)),
            pl.BlockSpec((1, LANES), lambda i, e: (0, 0)),
            pl.BlockSpec((1, d, D_EXPERT), lambda i, e: (e, 0, 0)),
            pl.BlockSpec((1, d, D_EXPERT), lambda i, e: (e, 0, 0)),
            pl.BlockSpec((1, D_EXPERT, d), lambda i, e: (e, 0, 0)),
        ],
        out_specs=pl.BlockSpec((tm, d), lambda i, e: (i, 0)),
        scratch_shapes=[pltpu.VMEM((tm, d), BF16), pltpu.VMEM((tm, LANES), F32)],
        compiler_params=_cparams("parallel", "arbitrary"),
        name="moe_dense",
    )(h, norm_ffn.reshape(1, d), w_router, b_router, w_gate, w_up, w_down)


def _ple_kernel(h_ref, p_ref, gp_ref, gf_ref, wg_ref, wp_ref, o_ref):
    h = h_ref[...]
    xn = _rms(h, gp_ref[...]).astype(BF16)
    gate = _sigmoid(jnp.dot(xn, wg_ref[...], preferred_element_type=F32))
    pp = jnp.dot(p_ref[...].astype(BF16), wp_ref[...], preferred_element_type=F32)
    o_ref[...] = _rms(h + pp * gate, gf_ref[...])


def ple_final(h, p, norm_ple, norm_final, w_gate, w_proj):
    m, d = h.shape
    pd = p.shape[1]
    tm = _tile(m, (512, 256, 128))
    return pl.pallas_call(
        _ple_kernel,
        out_shape=jax.ShapeDtypeStruct((m, d), F32),
        grid=(m // tm,),
        in_specs=[
            pl.BlockSpec((tm, d), lambda i: (i, 0)),
            pl.BlockSpec((tm, pd), lambda i: (i, 0)),
            pl.BlockSpec((1, d), lambda i: (0, 0)),
            pl.BlockSpec((1, d), lambda i: (0, 0)),
            pl.BlockSpec((d, d), lambda i: (0, 0)),
            pl.BlockSpec((pd, d), lambda i: (0, 0)),
        ],
        out_specs=pl.BlockSpec((tm, d), lambda i: (i, 0)),
        compiler_params=_cparams("parallel"),
        name="ple_final",
    )(h, p, norm_ple.reshape(1, d), norm_final.reshape(1, d), w_gate, w_proj)


def _rope_tables(pos):
    half = DK_B // 2
    inv = ROPE_THETA ** (-jnp.arange(half, dtype=F32) / half)
    ang = pos.astype(F32)[:, None] * inv[None, :]
    cos = jnp.cos(ang)
    sin = jnp.sin(ang)
    return jnp.concatenate([cos, cos], axis=-1), jnp.concatenate([-sin, sin], axis=-1)


def _prepare_weights(w_in, w_alpha2, b_alpha, w_proj_a, w_proj_b, w_out, w_router_group, b_router_group,
                     w_router_expert, b_router_expert, w_exp_gate, w_exp_up, w_exp_down, w_ple_gate, w_ple_proj):
    d = w_in.shape[0]
    sizes = [H_A * DK_A, H_A * DK_A, H_A * DV_A, H_A * DV_A, GATE_RANK,
             H_B * HB_W, H_B * HB_W, H_B * DV_B, 2 * d]
    off = np.concatenate([[0], np.cumsum(sizes)])
    w_gla = jnp.concatenate(
        [w_in[:, off[0]:off[4]], jnp.pad(w_in[:, off[4]:off[5]], ((0, 0), (0, LANES - GATE_RANK)))], axis=1)
    w_router = jnp.zeros((d, LANES), F32)
    w_router = w_router.at[:, :N_GROUPS].set(w_router_group)
    w_router = w_router.at[:, EXPERT_LANE0:EXPERT_LANE0 + N_EXPERTS].set(
        w_router_expert.transpose(1, 0, 2).reshape(d, N_EXPERTS))
    b_router = jnp.zeros((1, LANES), F32)
    b_router = b_router.at[0, :N_GROUPS].set(b_router_group)
    b_router = b_router.at[0, EXPERT_LANE0:EXPERT_LANE0 + N_EXPERTS].set(b_router_expert.reshape(N_EXPERTS))
    return dict(
        w_gla=w_gla.astype(BF16),
        w_q=w_in[:, off[5]:off[6]].astype(BF16),
        w_k=w_in[:, off[6]:off[7]].astype(BF16),
        w_v=w_in[:, off[7]:off[8]].astype(BF16),
        w_gm=w_in[:, off[8]:off[9]].astype(BF16),
        w_alpha2=w_alpha2, b_alpha=b_alpha,
        w_pa=w_proj_a.astype(BF16), w_pb=w_proj_b.astype(BF16), w_out=w_out.astype(BF16),
        w_router=w_router, b_router=b_router,
        w_eg=w_exp_gate.astype(BF16), w_eu=w_exp_up.astype(BF16), w_ed=w_exp_down.astype(BF16),
        w_pg=w_ple_gate.astype(BF16), w_pp=w_ple_proj.astype(BF16),
    )


def _layer(x, p_emb, pos, batch, chunk, s0, attend, W, norms, lam_vecs, za_dtype):
    norm_mix, gla_norm, diff_norm, norm_ffn, norm_ple, norm_final = norms
    hn = rmsnorm_cast(x, norm_mix)
    cos, sin = _rope_tables(pos)
    (za,) = matmul(hn, W["w_gla"], [za_dtype], name="in_proj_gla")
    (q,) = matmul_rope(hn, W["w_q"], cos, sin, [attend.q_dtype], scale=attend.q_scale, name="in_proj_q")
    k_outs = matmul_rope(hn, W["w_k"], cos, sin, [F32] + attend.kv_extra, name="in_proj_k")
    v_outs = matmul(hn, W["w_v"], [F32] + attend.kv_extra, name="in_proj_v")
    (sig_gm,) = matmul(hn, W["w_gm"], [BF16], act="sigmoid", name="in_proj_gates")

    oa, s_new = gla(za, W["w_alpha2"], W["b_alpha"], gla_norm, s0, batch, chunk)
    ob = attend(q, k_outs, v_outs, lam_vecs, diff_norm)
    mix = branch_mix(oa, ob, W["w_pa"], W["w_pb"], sig_gm)
    h1 = matmul_residual(mix, W["w_out"], x)
    h2 = moe_dense(h1, norm_ffn, W["w_router"], W["b_router"], W["w_eg"], W["w_eu"], W["w_ed"])
    y = ple_final(h2, p_emb, norm_ple, norm_final, W["w_pg"], W["w_pp"])
    return y, s_new, k_outs[0], v_outs[0]


class _PromptAttention:
    q_dtype = BF16
    kv_extra = [BF16]
    q_scale = DK_B ** -0.5 * LOG2_E

    def __init__(self, batch):
        self.batch = batch

    def __call__(self, q, k_outs, v_outs, lam_vecs, diff_norm):
        return flash_diff_attention(q, k_outs[1], v_outs[1], lam_vecs, diff_norm, self.batch)


class _PagedAttention:
    q_dtype = F32
    kv_extra = []
    q_scale = DK_B ** -0.5

    def __init__(self, batch, cache_k, cache_v, page_table):
        self.batch, self.cache_k, self.cache_v, self.page_table = batch, cache_k, cache_v, page_table

    def __call__(self, q, k_outs, v_outs, lam_vecs, diff_norm):
        b = self.batch
        t = q.shape[0] // b
        out = paged_diff_attention(q.reshape(b, t, -1), k_outs[0].reshape(b, t, -1), v_outs[0].reshape(b, t, -1),
                                   self.cache_k, self.cache_v, self.page_table, lam_vecs, diff_norm)
        return out.reshape(b * t, -1).astype(BF16)


def kernel(x_prompt, x_sample, cache_k, cache_v, state_gla, page_table, p_prompt, p_sample, norm_mix, w_in, w_alpha2, b_alpha, gla_norm, w_proj_a, lambda_q1, lambda_k1, lambda_q2, lambda_k2, diff_norm, w_proj_b, w_out, norm_ffn, w_router_group, b_router_group, w_router_expert, b_router_expert, w_exp_gate, w_exp_up, w_exp_down, norm_ple, w_ple_gate, w_ple_proj, norm_final):
    assert w_in.shape[0] == 1, "one layer"
    bp, tp, d = x_prompt.shape
    bs, ts, _ = x_sample.shape
    n_pool, page = cache_k.shape[1], cache_k.shape[2]
    past_len = page_table.shape[1] * page

    W = _prepare_weights(w_in[0], w_alpha2[0], b_alpha[0], w_proj_a[0], w_proj_b[0], w_out[0],
                         w_router_group[0], b_router_group[0], w_router_expert[0], b_router_expert[0],
                         w_exp_gate[0], w_exp_up[0], w_exp_down[0], w_ple_gate[0], w_ple_proj[0])
    norms = (norm_mix[0], gla_norm[0], diff_norm[0], norm_ffn[0], norm_ple[0], norm_final)
    lam_vecs = jnp.stack([lambda_q1[0], lambda_k1[0], lambda_q2[0], lambda_k2[0]])

    pos_p = jnp.tile(jnp.arange(tp, dtype=jnp.int32), bp)
    yp, sp, kp, vp = _layer(x_prompt.reshape(bp * tp, d), p_prompt[0].reshape(bp * tp, -1), pos_p, bp,
                            min(GLA_CHUNK, tp), None, _PromptAttention(bp), W, norms, lam_vecs, BF16)

    pos_s = jnp.tile(past_len + jnp.arange(ts, dtype=jnp.int32), bs)
    attend_s = _PagedAttention(bs, cache_k.reshape(n_pool, page, -1), cache_v.reshape(n_pool, page, -1), page_table)
    ys, ss, ks, vs = _layer(x_sample.reshape(bs * ts, d), p_sample[0].reshape(bs * ts, -1), pos_s, bs,
                            ts, state_gla[0], attend_s, W, norms, lam_vecs, F32)

    return (yp.reshape(bp, tp, d), ys.reshape(bs, ts, d),
            sp[None], kp.reshape(1, bp, tp, H_B, 2, DK_B), vp.reshape(1, bp, tp, H_B, DV_B),
            ss[None], ks.reshape(1, bs, ts, H_B, 2, DK_B), vs.reshape(1, bs, ts, H_B, DV_B))
```

```python
import functools

import jax
import jax.numpy as jnp
import numpy as np
from jax import lax
from jax.experimental import pallas as pl
from jax.experimental.pallas import tpu as pltpu

F32 = jnp.float32
BF16 = jnp.bfloat16

H_A, DK_A, DV_A = 4, 128, 256
GATE_RANK = 16
GATE_TAU = 16.0
GLA_CHUNK = 64
H_B, DK_B, DV_B = 8, 128, 256
ROPE_THETA = 10000.0
N_GROUPS, EXPERTS_PER_GROUP, D_EXPERT = 4, 8, 256
N_EXPERTS = N_GROUPS * EXPERTS_PER_GROUP
EPS = 1e-6
LAM_INIT = 0.8 - 0.6

LANES = 128
V7X_VMEM_LIMIT = 56 * 1024 * 1024

HB_W = 2 * DK_B
FLASH_TQ, FLASH_TK = 1024, 1024
FLASH_RQ = 128
LOG2_E = 1.4426950408889634
PAGES_PER_STEP = 4
EXPERT_LANE0 = 32
NEG_INF = float("-inf")


def _cparams(*sem):
    return pltpu.CompilerParams(dimension_semantics=sem, vmem_limit_bytes=V7X_VMEM_LIMIT)


def _tile(n, candidates):
    for c in candidates:
        if n % c == 0:
            return c
    return n


def _sigmoid(x):
    return 1.0 / (1.0 + jnp.exp(-x))


def _rms(x, g):
    ms = jnp.mean(x * x, axis=-1, keepdims=True)
    return x * lax.rsqrt(ms + EPS) * g


def _rmsnorm_kernel(x_ref, g_ref, o_ref):
    o_ref[...] = _rms(x_ref[...], g_ref[...]).astype(o_ref.dtype)


def rmsnorm_cast(x, g, out_dtype=BF16):
    m, d = x.shape
    tm = _tile(m, (512, 256, 128, 8))
    return pl.pallas_call(
        _rmsnorm_kernel,
        out_shape=jax.ShapeDtypeStruct((m, d), out_dtype),
        grid=(m // tm,),
        in_specs=[pl.BlockSpec((tm, d), lambda i: (i, 0)), pl.BlockSpec((1, d), lambda i: (0, 0))],
        out_specs=pl.BlockSpec((tm, d), lambda i: (i, 0)),
        compiler_params=_cparams("parallel"),
        name="rmsnorm_cast",
    )(x, g.reshape(1, d))


def _mm_kernel(x_ref, w_ref, *o_refs, act):
    acc = jnp.dot(x_ref[...], w_ref[...], preferred_element_type=F32)
    if act == "sigmoid":
        acc = _sigmoid(acc)
    for o_ref in o_refs:
        o_ref[...] = acc.astype(o_ref.dtype)


def matmul(x, w, out_dtypes, act=None, tn_candidates=(1024, 640, 512, 256, 128), name="matmul"):
    m, k = x.shape
    n = w.shape[1]
    tm = _tile(m, (1024, 512, 256, 128))
    tn = _tile(n, tn_candidates)
    outs = pl.pallas_call(
        functools.partial(_mm_kernel, act=act),
        out_shape=[jax.ShapeDtypeStruct((m, n), dt) for dt in out_dtypes],
        grid=(m // tm, n // tn),
        in_specs=[pl.BlockSpec((tm, k), lambda i, j: (i, 0)), pl.BlockSpec((k, tn), lambda i, j: (0, j))],
        out_specs=[pl.BlockSpec((tm, tn), lambda i, j: (i, j)) for _ in out_dtypes],
        compiler_params=_cparams("parallel", "parallel"),
        name=name,
    )(x, w)
    return outs


def _mm_rope_kernel(x_ref, w_ref, cos_ref, sin_ref, *o_refs, scale):
    acc = jnp.dot(x_ref[...], w_ref[...], preferred_element_type=F32)
    cos = cos_ref[...]
    sin = sin_ref[...]
    half = DK_B // 2
    parts = []
    for c in range(acc.shape[1] // DK_B):
        a = acc[:, c * DK_B:(c + 1) * DK_B]
        parts.append(a * cos + pltpu.roll(a, half, axis=1) * sin)
    rot = jnp.concatenate(parts, axis=1)
    if scale != 1.0:
        rot = rot * scale
    for o_ref in o_refs:
        o_ref[...] = rot.astype(o_ref.dtype)


def matmul_rope(x, w, cos, sin, out_dtypes, scale=1.0, name="matmul_rope"):
    m, k = x.shape
    n = w.shape[1]
    tm = _tile(m, (1024, 512, 256, 128))
    tn = _tile(n, (1024, 512, 256, 128))
    return pl.pallas_call(
        functools.partial(_mm_rope_kernel, scale=scale),
        out_shape=[jax.ShapeDtypeStruct((m, n), dt) for dt in out_dtypes],
        grid=(m // tm, n // tn),
        in_specs=[
            pl.BlockSpec((tm, k), lambda i, j: (i, 0)),
            pl.BlockSpec((k, tn), lambda i, j: (0, j)),
            pl.BlockSpec((tm, DK_B), lambda i, j: (i, 0)),
            pl.BlockSpec((tm, DK_B), lambda i, j: (i, 0)),
        ],
        out_specs=[pl.BlockSpec((tm, tn), lambda i, j: (i, j)) for _ in out_dtypes],
        compiler_params=_cparams("parallel", "parallel"),
        name=name,
    )(x, w, cos, sin)


def _mix_kernel(oa_ref, ob_ref, wa_ref, wb_ref, sa_ref, sb_ref, o_ref):
    ya = jnp.dot(oa_ref[...], wa_ref[...], preferred_element_type=F32)
    yb = jnp.dot(ob_ref[...], wb_ref[...], preferred_element_type=F32)
    mix = sa_ref[...].astype(F32) * ya + sb_ref[...].astype(F32) * yb
    o_ref[...] = mix.astype(o_ref.dtype)


def branch_mix(oa, ob, w_pa, w_pb, sig_gm):
    m, ka = oa.shape
    kb = ob.shape[1]
    d = w_pa.shape[1]
    tm = _tile(m, (1024, 512, 256, 128))
    tn = _tile(d, (1024, 512, 256, 128))
    nb = d // tn
    return pl.pallas_call(
        _mix_kernel,
        out_shape=jax.ShapeDtypeStruct((m, d), BF16),
        grid=(m // tm, nb),
        in_specs=[
            pl.BlockSpec((tm, ka), lambda i, j: (i, 0)),
            pl.BlockSpec((tm, kb), lambda i, j: (i, 0)),
            pl.BlockSpec((ka, tn), lambda i, j: (0, j)),
            pl.BlockSpec((kb, tn), lambda i, j: (0, j)),
            pl.BlockSpec((tm, tn), lambda i, j: (i, j)),
            pl.BlockSpec((tm, tn), lambda i, j: (i, j + nb)),
        ],
        out_specs=pl.BlockSpec((tm, tn), lambda i, j: (i, j)),
        compiler_params=_cparams("parallel", "parallel"),
        name="branch_mix",
    )(oa, ob, w_pa, w_pb, sig_gm, sig_gm)


def _mm_res_kernel(x_ref, w_ref, r_ref, o_ref):
    o_ref[...] = r_ref[...] + jnp.dot(x_ref[...], w_ref[...], preferred_element_type=F32)


def matmul_residual(x, w, res):
    m, k = x.shape
    n = w.shape[1]
    tm = _tile(m, (1024, 512, 256, 128))
    tn = _tile(n, (1024, 512, 256, 128))
    return pl.pallas_call(
        _mm_res_kernel,
        out_shape=jax.ShapeDtypeStruct((m, n), F32),
        grid=(m // tm, n // tn),
        in_specs=[
            pl.BlockSpec((tm, k), lambda i, j: (i, 0)),
            pl.BlockSpec((k, tn), lambda i, j: (0, j)),
            pl.BlockSpec((tm, tn), lambda i, j: (i, j)),
        ],
        out_specs=pl.BlockSpec((tm, tn), lambda i, j: (i, j)),
        compiler_params=_cparams("parallel", "parallel"),
        name="out_proj_residual",
    )(x, w, res)


def _gla_kernel(q_ref, k_ref, v_ref, r_ref, ga_ref, wal_ref, bal_ref, gn_ref, *rest, has_s0, cp):
    if has_s0:
        s0_ref, o_ref, s_out_ref, st_ref = rest
    else:
        o_ref, s_out_ref, st_ref = rest
    ci = pl.program_id(1)
    c = q_ref.shape[1]

    @pl.when(ci == 0)
    def _init():
        for h in range(H_A):
            if has_s0:
                st_ref[h] = s0_ref[0, h].T
            else:
                st_ref[h] = jnp.zeros(st_ref.shape[1:], F32)

    def padded(x):
        x = x.astype(F32)
        if cp == c:
            return x
        return jnp.concatenate([x, jnp.zeros((cp - c, x.shape[1]), F32)], axis=0)

    row = lax.broadcasted_iota(jnp.int32, (cp, cp), 0)
    col = lax.broadcasted_iota(jnp.int32, (cp, cp), 1)
    causal = row >= col
    tril = causal.astype(F32)
    ga = padded(ga_ref[0]).astype(BF16)
    x = jnp.dot(ga, wal_ref[...], preferred_element_type=F32) + bal_ref[...]
    logf = (jnp.minimum(x, 0.0) - jnp.log(1.0 + jnp.exp(-jnp.abs(x)))) * (1.0 / GATE_TAU)
    if cp != c:
        logf = jnp.where(lax.broadcasted_iota(jnp.int32, logf.shape, 0) < c, logf, 0.0)
    b_all = jnp.dot(tril, logf, precision=lax.Precision.HIGHEST, preferred_element_type=F32)
    outs = []
    for h in range(H_A):
        q = padded(q_ref[0, :, h * DK_A:(h + 1) * DK_A]) * (DK_A ** -0.5)
        k = padded(k_ref[0, :, h * DK_A:(h + 1) * DK_A])
        v = padded(v_ref[0, :, h * DV_A:(h + 1) * DV_A]).astype(BF16)
        b = b_all[:, h * DK_A:(h + 1) * DK_A]
        b_last = b[cp - 1:cp, :]

        qd = (q * jnp.exp(b)).astype(BF16)
        kd = (k * jnp.exp(-b)).astype(BF16)
        ke = (k * jnp.exp(b_last - b)).astype(BF16)

        a = lax.dot_general(qd, kd, (((1,), (1,)), ((), ())), preferred_element_type=F32)
        a = jnp.where(causal, a, 0.0)
        st = st_ref[h]
        o = jnp.dot(a.astype(BF16), v, preferred_element_type=F32)
        o = o + lax.dot_general(qd, st.astype(BF16), (((1,), (1,)), ((), ())), preferred_element_type=F32)
        kv_t = lax.dot_general(v, ke, (((0,), (0,)), ((), ())), preferred_element_type=F32)
        st_ref[h] = st * jnp.exp(b_last) + kv_t

        r = r_ref[0, :, h * DV_A:(h + 1) * DV_A].astype(F32)
        outs.append(_rms(o[:c], gn_ref[...]) * (r * _sigmoid(r)))
    o_ref[0] = jnp.concatenate(outs, axis=1).astype(o_ref.dtype)

    @pl.when(ci == pl.num_programs(1) - 1)
    def _final():
        for h in range(H_A):
            s_out_ref[0, h] = st_ref[h].T


def gla(za, w_alpha2, b_alpha, gla_norm, s0, batch, chunk):
    m = za.shape[0]
    t = m // batch
    n_chunks = t // chunk
    cp = chunk if chunk % 16 == 0 else LANES
    za3 = za.reshape(batch, t, za.shape[1])
    wal = jnp.pad(w_alpha2, ((0, LANES - GATE_RANK), (0, 0))).astype(BF16)
    bal = b_alpha.reshape(1, H_A * DK_A)
    qk_w = H_A * DK_A
    vr_w = H_A * DV_A
    gcol = (2 * qk_w + 2 * vr_w) // LANES
    in_specs = [
        pl.BlockSpec((1, chunk, qk_w), lambda b, c: (b, c, 0)),
        pl.BlockSpec((1, chunk, qk_w), lambda b, c: (b, c, 1)),
        pl.BlockSpec((1, chunk, vr_w), lambda b, c: (b, c, 1)),
        pl.BlockSpec((1, chunk, vr_w), lambda b, c: (b, c, 2)),
        pl.BlockSpec((1, chunk, LANES), lambda b, c: (b, c, gcol)),
        pl.BlockSpec((LANES, qk_w), lambda b, c: (0, 0)),
        pl.BlockSpec((1, qk_w), lambda b, c: (0, 0)),
        pl.BlockSpec((1, DV_A), lambda b, c: (0, 0)),
    ]
    args = [za3, za3, za3, za3, za3, wal, bal, gla_norm.reshape(1, DV_A)]
    if s0 is not None:
        in_specs.append(pl.BlockSpec((1, H_A, DK_A, DV_A), lambda b, c: (b, 0, 0, 0)))
        args.append(s0)
    oa, s_new = pl.pallas_call(
        functools.partial(_gla_kernel, has_s0=s0 is not None, cp=cp),
        out_shape=[
            jax.ShapeDtypeStruct((batch, t, vr_w), BF16 if cp == chunk else F32),
            jax.ShapeDtypeStruct((batch, H_A, DK_A, DV_A), F32),
        ],
        grid=(batch, n_chunks),
        in_specs=in_specs,
        out_specs=[
            pl.BlockSpec((1, chunk, vr_w), lambda b, c: (b, c, 0)),
            pl.BlockSpec((1, H_A, DK_A, DV_A), lambda b, c: (b, 0, 0, 0)),
        ],
        scratch_shapes=[pltpu.VMEM((H_A, DV_A, DK_A), F32)],
        compiler_params=_cparams("parallel", "arbitrary"),
        name="gla",
    )(*args)
    return oa.reshape(m, vr_w).astype(BF16), s_new


def _lambda(lam_ref):
    lv = lam_ref[...]
    s1 = jnp.sum(lv[0:1] * lv[1:2], axis=-1, keepdims=True)
    s2 = jnp.sum(lv[2:3] * lv[3:4], axis=-1, keepdims=True)
    return jnp.exp(s1) - jnp.exp(s2) + LAM_INIT


def _lane_tile(x, width):
    return x if width == LANES else jnp.concatenate([x] * (width // LANES), axis=1)


def _flash_kernel(qi_ref, kj_ref, q_ref, k_ref, v_ref, lam_ref, dn_ref, o_ref,
                  m_ref, l_ref, acc_ref, *, tq, tk, rq):
    step = pl.program_id(2)
    qi = qi_ref[step]
    kj = kj_ref[step]

    @pl.when(kj == 0)
    def _init():
        m_ref[...] = jnp.full_like(m_ref, NEG_INF)
        l_ref[...] = jnp.zeros_like(l_ref)
        acc_ref[...] = jnp.zeros_like(acc_ref)

    def update(masked):
        v = v_ref[0]
        for r0 in range(0, tq, rq):
            if masked:
                q_pos = qi * tq + r0 + lax.broadcasted_iota(jnp.int32, (rq, tk), 0)
                k_pos = kj * tk + lax.broadcasted_iota(jnp.int32, (rq, tk), 1)
                visible = k_pos <= q_pos
            for mi in range(2):
                q = q_ref[0, r0:r0 + rq, mi * DK_B:(mi + 1) * DK_B]
                k = k_ref[0, :, mi * DK_B:(mi + 1) * DK_B]
                s = lax.dot_general(q, k, (((1,), (1,)), ((), ())), preferred_element_type=F32)
                if masked:
                    s = jnp.where(visible, s, NEG_INF)
                m_prev = m_ref[mi, r0:r0 + rq, :]
                m_new = jnp.maximum(m_prev, jnp.max(s, axis=-1, keepdims=True))
                alpha = jnp.exp2(m_prev - m_new)
                p = jnp.exp2(s - _lane_tile(m_new, tk))
                l_ref[mi, r0:r0 + rq, :] = alpha * l_ref[mi, r0:r0 + rq, :] + jnp.sum(p, axis=-1, keepdims=True)
                pv = jnp.dot(p.astype(BF16), v, preferred_element_type=F32)
                acc_ref[mi, r0:r0 + rq, :] = _lane_tile(alpha, DV_B) * acc_ref[mi, r0:r0 + rq, :] + pv
                m_ref[mi, r0:r0 + rq, :] = m_new

    on_diagonal = (kj + 1) * tk - 1 > qi * tq

    @pl.when(on_diagonal)
    def _masked():
        update(True)

    @pl.when(jnp.logical_not(on_diagonal))
    def _unmasked():
        update(False)

    last_kj = ((qi + 1) * tq - 1) // tk

    @pl.when(kj == last_kj)
    def _final():
        lam = _lambda(lam_ref)
        o = (acc_ref[0] / _lane_tile(l_ref[0], DV_B)
             - lam * (acc_ref[1] / _lane_tile(l_ref[1], DV_B)))
        o_ref[0] = (_rms(o, dn_ref[...]) * (1.0 - LAM_INIT)).astype(o_ref.dtype)


def flash_diff_attention(q, k, v, lam_vecs, diff_norm, batch):
    m = q.shape[0]
    t = m // batch
    tq = min(FLASH_TQ, t)
    tk = min(FLASH_TK, t)
    pairs = [(i, j) for i in range(t // tq) for j in range(((i + 1) * tq - 1) // tk + 1)]
    qi = jnp.asarray(np.array([p[0] for p in pairs], np.int32))
    kj = jnp.asarray(np.array([p[1] for p in pairs], np.int32))
    q3 = q.reshape(batch, t, H_B * HB_W)
    k3 = k.reshape(batch, t, H_B * HB_W)
    v3 = v.reshape(batch, t, H_B * DV_B)
    out = pl.pallas_call(
        functools.partial(_flash_kernel, tq=tq, tk=tk, rq=min(FLASH_RQ, tq)),
        out_shape=jax.ShapeDtypeStruct((batch, t, H_B * DV_B), BF16),
        grid_spec=pltpu.PrefetchScalarGridSpec(
            num_scalar_prefetch=2,
            grid=(batch, H_B, len(pairs)),
            in_specs=[
                pl.BlockSpec((1, tq, HB_W), lambda b, h, s, qi, kj: (b, qi[s], h)),
                pl.BlockSpec((1, tk, HB_W), lambda b, h, s, qi, kj: (b, kj[s], h)),
                pl.BlockSpec((1, tk, DV_B), lambda b, h, s, qi, kj: (b, kj[s], h)),
                pl.BlockSpec((4, DK_B), lambda b, h, s, qi, kj: (0, 0)),
                pl.BlockSpec((1, DV_B), lambda b, h, s, qi, kj: (0, 0)),
            ],
            out_specs=pl.BlockSpec((1, tq, DV_B), lambda b, h, s, qi, kj: (b, qi[s], h)),
            scratch_shapes=[
                pltpu.VMEM((2, tq, LANES), F32), pltpu.VMEM((2, tq, LANES), F32),
                pltpu.VMEM((2, tq, DV_B), F32),
            ],
        ),
        compiler_params=_cparams("parallel", "parallel", "arbitrary"),
        name="flash_diff_attention",
    )(qi, kj, q3, k3, v3, lam_vecs, diff_norm.reshape(1, DV_B))
    return out.reshape(m, H_B * DV_B)


def _rows_by_head(x, col0, width, stride):
    return jnp.concatenate([x[:, h * stride + col0:h * stride + col0 + width] for h in range(H_B)], axis=0)


def _paged_kernel(pt_ref, q_ref, kn_ref, vn_ref, lam_ref, dn_ref, *rest, pps, t_new):
    k_refs = rest[:pps]
    v_refs = rest[pps:2 * pps]
    o_ref = rest[2 * pps]
    qm, kb, vb, mask_ref, m_ref, l_ref, acc_ref = rest[2 * pps + 1:]
    j = pl.program_id(1)
    hq = H_B * t_new
    rows_pg = v_refs[0].shape[1]

    @pl.when(j == 0)
    def _init():
        q = q_ref[0]
        for m in range(2):
            qm[m] = _rows_by_head(q, m * DK_B, DK_B, HB_W).astype(BF16)
        row = lax.broadcasted_iota(jnp.int32, mask_ref.shape, 0)
        col = lax.broadcasted_iota(jnp.int32, mask_ref.shape, 1)
        mask_ref[...] = jnp.where(col % H_B == row // t_new, 0.0, NEG_INF)
        m_ref[...] = jnp.full_like(m_ref, NEG_INF)
        l_ref[...] = jnp.zeros_like(l_ref)
        acc_ref[...] = jnp.zeros_like(acc_ref)

    def attend(k_maps, v, mask):
        width = v.shape[0]
        s = jnp.concatenate(
            [lax.dot_general(qm[m], k_maps[m], (((1,), (1,)), ((), ())), preferred_element_type=F32) + mask
             for m in range(2)], axis=0)
        m_prev = m_ref[...]
        m_new = jnp.maximum(m_prev, jnp.max(s, axis=-1, keepdims=True))
        alpha = jnp.exp(m_prev - m_new)
        p = jnp.exp(s - _lane_tile(m_new, width))
        l_ref[...] = alpha * l_ref[...] + jnp.sum(p, axis=-1, keepdims=True)
        acc_ref[...] = (_lane_tile(alpha, DV_B) * acc_ref[...]
                        + jnp.dot(p.astype(BF16), v, preferred_element_type=F32))
        m_ref[...] = m_new

    for p in range(pps):
        for m in range(2):
            kb[m, p * rows_pg:(p + 1) * rows_pg, :] = k_refs[p][0, pl.ds(m, rows_pg, stride=2), :].astype(BF16)
        vb[p * rows_pg:(p + 1) * rows_pg, :] = v_refs[p][0].astype(BF16)
    attend([kb[0], kb[1]], vb[...], mask_ref[...])

    @pl.when(j == pl.num_programs(1) - 1)
    def _final():
        kn = kn_ref[0]
        vn = vn_ref[0]
        k_new = [jnp.concatenate([_rows_by_head(kn, m * DK_B, DK_B, HB_W),
                                  jnp.zeros((LANES - hq, DK_B), F32)], axis=0).astype(BF16) for m in range(2)]
        v_new = jnp.concatenate([_rows_by_head(vn, 0, DV_B, DV_B),
                                 jnp.zeros((LANES - hq, DV_B), F32)], axis=0).astype(BF16)
        row = lax.broadcasted_iota(jnp.int32, (hq, LANES), 0)
        col = lax.broadcasted_iota(jnp.int32, (hq, LANES), 1)
        visible = (col // t_new == row // t_new) & (col % t_new <= row % t_new)
        attend(k_new, v_new, jnp.where(visible, 0.0, NEG_INF))

        lam = _lambda(lam_ref)
        w = acc_ref[...] / _lane_tile(l_ref[...], DV_B)
        heads = []
        for h in range(H_B):
            w1 = w[h * t_new:(h + 1) * t_new]
            w2 = w[hq + h * t_new:hq + (h + 1) * t_new]
            heads.append(_rms(w1 - lam * w2, dn_ref[...]) * (1.0 - LAM_INIT))
        o_ref[0] = jnp.concatenate(heads, axis=1).astype(o_ref.dtype)


def paged_diff_attention(q, k_new, v_new, cache_k, cache_v, page_table, lam_vecs, diff_norm):
    db, t_new, width = q.shape
    n_pages = page_table.shape[1]
    pps = min(PAGES_PER_STEP, n_pages)
    hq = H_B * t_new
    rows_pg = cache_v.shape[1]
    assert n_pages % pps == 0 and hq <= LANES and cache_k.shape[1] == 2 * rows_pg

    def k_spec(p):
        return pl.BlockSpec((1, 2 * rows_pg, DK_B), lambda b, j, pt: (pt[b, j * pps + p], 0, 0))

    def v_spec(p):
        return pl.BlockSpec((1, rows_pg, DV_B), lambda b, j, pt: (pt[b, j * pps + p], 0, 0))

    seq_spec = pl.BlockSpec((1, t_new, width), lambda b, j, pt: (b, 0, 0))
    return pl.pallas_call(
        functools.partial(_paged_kernel, pps=pps, t_new=t_new),
        out_shape=jax.ShapeDtypeStruct((db, t_new, H_B * DV_B), F32),
        grid_spec=pltpu.PrefetchScalarGridSpec(
            num_scalar_prefetch=1,
            grid=(db, n_pages // pps),
            in_specs=[seq_spec, seq_spec, seq_spec,
                      pl.BlockSpec((4, DK_B), lambda b, j, pt: (0, 0)),
                      pl.BlockSpec((1, DV_B), lambda b, j, pt: (0, 0))]
            + [k_spec(p) for p in range(pps)] + [v_spec(p) for p in range(pps)],
            out_specs=pl.BlockSpec((1, t_new, H_B * DV_B), lambda b, j, pt: (b, 0, 0)),
            scratch_shapes=[
                pltpu.VMEM((2, hq, DK_B), BF16),
                pltpu.VMEM((2, pps * rows_pg, DK_B), BF16),
                pltpu.VMEM((pps * rows_pg, DV_B), BF16),
                pltpu.VMEM((hq, pps * rows_pg), F32),
                pltpu.VMEM((2 * hq, LANES), F32), pltpu.VMEM((2 * hq, LANES), F32),
                pltpu.VMEM((2 * hq, DV_B), F32),
            ],
        ),
        compiler_params=_cparams("parallel", "arbitrary"),
        name="paged_diff_attention",
    )(page_table, q, k_new, v_new, lam_vecs, diff_norm.reshape(1, DV_B),
      *([cache_k] * pps), *([cache_v] * pps))


def _split_bf16(x):
    hi = x.astype(BF16)
    lo = (x - hi.astype(F32)).astype(BF16)
    return hi, lo


def _route(logits):
    lane = lax.broadcasted_iota(jnp.int32, logits.shape, 1)
    lane_f = lane.astype(F32)
    big = float(LANES)
    lg = jnp.where(lane < N_GROUPS, logits, NEG_INF)
    mg = jnp.max(lg, axis=-1, keepdims=True)
    g_idx = jnp.min(jnp.where(lg == mg, lane_f, big), axis=-1, keepdims=True)
    pg_top = 1.0 / jnp.sum(jnp.exp(lg - mg), axis=-1, keepdims=True)
    lo = EXPERT_LANE0 + g_idx * EXPERTS_PER_GROUP
    sel = (lane_f >= lo) & (lane_f < lo + EXPERTS_PER_GROUP)
    le = jnp.where(sel, logits, NEG_INF)
    m1 = jnp.max(le, axis=-1, keepdims=True)
    i1 = jnp.min(jnp.where(le == m1, lane_f, big), axis=-1, keepdims=True)
    le2 = jnp.where(lane_f == i1, NEG_INF, le)
    m2 = jnp.max(le2, axis=-1, keepdims=True)
    i2 = jnp.min(jnp.where(le2 == m2, lane_f, big), axis=-1, keepdims=True)
    e2 = jnp.exp(m2 - m1)
    w1 = pg_top / (1.0 + e2)
    w2 = pg_top * e2 / (1.0 + e2)
    return jnp.where(lane_f == i1, w1, 0.0) + jnp.where(lane_f == i2, w2, 0.0)


def _moe_kernel(h_ref, g_ref, wr_ref, br_ref, wg_ref, wu_ref, wd_ref, o_ref, x_ref, gate_ref):
    e = pl.program_id(1)

    @pl.when(e == 0)
    def _init():
        h = h_ref[...]
        x = _rms(h, g_ref[...])
        x_ref[...] = x.astype(BF16)
        x_hi, x_lo = _split_bf16(x)
        w_hi, w_lo = _split_bf16(wr_ref[...])
        logits = (jnp.dot(x_hi, w_hi, preferred_element_type=F32)
                  + jnp.dot(x_lo, w_hi, preferred_element_type=F32)
                  + jnp.dot(x_hi, w_lo, preferred_element_type=F32)) + br_ref[...]
        gate_ref[...] = _route(logits)
        o_ref[...] = h

    x = x_ref[...]
    hg = jnp.dot(x, wg_ref[0], preferred_element_type=F32)
    hu = jnp.dot(x, wu_ref[0], preferred_element_type=F32)
    lane = lax.broadcasted_iota(jnp.int32, gate_ref.shape, 1)
    gcol = jnp.sum(jnp.where(lane == e + EXPERT_LANE0, gate_ref[...], 0.0), axis=-1, keepdims=True)
    act = (hg * _sigmoid(hg)) * hu * gcol
    o_ref[...] += jnp.dot(act.astype(BF16), wd_ref[0], preferred_element_type=F32)


def moe_dense(h, norm_ffn, w_router, b_router, w_gate, w_up, w_down):
    m, d = h.shape
    tm = _tile(m, (1024, 512, 256, 128))
    return pl.pallas_call(
        _moe_kernel,
        out_shape=jax.ShapeDtypeStruct((m, d), F32),
        grid=(m // tm, N_EXPERTS),
        in_specs=[
            pl.BlockSpec((tm, d), lambda i, e: (i, 0)),
            pl.BlockSpec((1, d), lambda i, e: (0, 0)),
            pl.BlockSpec((d, LANES), lambda i, e: (0, 0)),
            pl.BlockSpec((1, LANES), lambda i, e: (0, 0)),
            pl.BlockSpec((1, d, D_EXPERT), lambda i, e: (e, 0, 0)),
            pl.BlockSpec((1, d, D_EXPERT), lambda i, e: (e, 0, 0)),
            pl.BlockSpec((1, D_EXPERT, d), lambda i, e: (e, 0, 0)),
        ],
        out_specs=pl.BlockSpec((tm, d), lambda i, e: (i, 0)),
        scratch_shapes=[pltpu.VMEM((tm, d), BF16), pltpu.VMEM((tm, LANES), F32)],
        compiler_params=_cparams("parallel", "arbitrary"),
        name="moe_dense",
    )(h, norm_ffn.reshape(1, d), w_router, b_router, w_gate, w_up, w_down)


def _ple_kernel(h_ref, p_ref, gp_ref, gf_ref, wg_ref, wp_ref, o_ref):
    h = h_ref[...]
    xn = _rms(h, gp_ref[...]).astype(BF16)
    gate = _sigmoid(jnp.dot(xn, wg_ref[...], preferred_element_type=F32))
    pp = jnp.dot(p_ref[...].astype(BF16), wp_ref[...], preferred_element_type=F32)
    o_ref[...] = _rms(h + pp * gate, gf_ref[...])


def ple_final(h, p, norm_ple, norm_final, w_gate, w_proj):
    m, d = h.shape
    pd = p.shape[1]
    tm = _tile(m, (512, 256, 128))
    return pl.pallas_call(
        _ple_kernel,
        out_shape=jax.ShapeDtypeStruct((m, d), F32),
        grid=(m // tm,),
        in_specs=[
            pl.BlockSpec((tm, d), lambda i: (i, 0)),
            pl.BlockSpec((tm, pd), lambda i: (i, 0)),
            pl.BlockSpec((1, d), lambda i: (0, 0)),
            pl.BlockSpec((1, d), lambda i: (0, 0)),
            pl.BlockSpec((d, d), lambda i: (0, 0)),
            pl.BlockSpec((pd, d), lambda i: (0, 0)),
        ],
        out_specs=pl.BlockSpec((tm, d), lambda i: (i, 0)),
        compiler_params=_cparams("parallel"),
        name="ple_final",
    )(h, p, norm_ple.reshape(1, d), norm_final.reshape(1, d), w_gate, w_proj)


def _rope_tables(pos):
    half = DK_B // 2
    inv = ROPE_THETA ** (-jnp.arange(half, dtype=F32) / half)
    ang = pos.astype(F32)[:, None] * inv[None, :]
    cos = jnp.cos(ang)
    sin = jnp.sin(ang)
    return jnp.concatenate([cos, cos], axis=-1), jnp.concatenate([-sin, sin], axis=-1)


def _prepare_weights(w_in, w_alpha2, b_alpha, w_proj_a, w_proj_b, w_out, w_router_group, b_router_group,
                     w_router_expert, b_router_expert, w_exp_gate, w_exp_up, w_exp_down, w_ple_gate, w_ple_proj):
    d = w_in.shape[0]
    sizes = [H_A * DK_A, H_A * DK_A, H_A * DV_A, H_A * DV_A, GATE_RANK,
             H_B * HB_W, H_B * HB_W, H_B * DV_B, 2 * d]
    off = np.concatenate([[0], np.cumsum(sizes)])
    w_gla = jnp.concatenate(
        [w_in[:, off[0]:off[4]], jnp.pad(w_in[:, off[4]:off[5]], ((0, 0), (0, LANES - GATE_RANK)))], axis=1)
    w_router = jnp.zeros((d, LANES), F32)
    w_router = w_router.at[:, :N_GROUPS].set(w_router_group)
    w_router = w_router.at[:, EXPERT_LANE0:EXPERT_LANE0 + N_EXPERTS].set(
        w_router_expert.transpose(1, 0, 2).reshape(d, N_EXPERTS))
    b_router = jnp.zeros((1, LANES), F32)
    b_router = b_router.at[0, :N_GROUPS].set(b_router_group)
    b_router = b_router.at[0, EXPERT_LANE0:EXPERT_LANE0 + N_EXPERTS].set(b_router_expert.reshape(N_EXPERTS))
    return dict(
        w_gla=w_gla.astype(BF16),
        w_q=w_in[:, off[5]:off[6]].astype(BF16),
        w_k=w_in[:, off[6]:off[7]].astype(BF16),
        w_v=w_in[:, off[7]:off[8]].astype(BF16),
        w_gm=w_in[:, off[8]:off[9]].astype(BF16),
        w_alpha2=w_alpha2, b_alpha=b_alpha,
        w_pa=w_proj_a.astype(BF16), w_pb=w_proj_b.astype(BF16), w_out=w_out.astype(BF16),
        w_router=w_router, b_router=b_router,
        w_eg=w_exp_gate.astype(BF16), w_eu=w_exp_up.astype(BF16), w_ed=w_exp_down.astype(BF16),
        w_pg=w_ple_gate.astype(BF16), w_pp=w_ple_proj.astype(BF16),
    )


def _layer(x, p_emb, pos, batch, chunk, s0, attend, W, norms, lam_vecs, za_dtype):
    norm_mix, gla_norm, diff_norm, norm_ffn, norm_ple, norm_final = norms
    hn = rmsnorm_cast(x, norm_mix)
    cos, sin = _rope_tables(pos)
    (za,) = matmul(hn, W["w_gla"], [za_dtype], name="in_proj_gla")
    (q,) = matmul_rope(hn, W["w_q"], cos, sin, [attend.q_dtype], scale=attend.q_scale, name="in_proj_q")
    k_outs = matmul_rope(hn, W["w_k"], cos, sin, [F32] + attend.kv_extra, name="in_proj_k")
    v_outs = matmul(hn, W["w_v"], [F32] + attend.kv_extra, name="in_proj_v")
    (sig_gm,) = matmul(hn, W["w_gm"], [BF16], act="sigmoid", name="in_proj_gates")

    oa, s_new = gla(za, W["w_alpha2"], W["b_alpha"], gla_norm, s0, batch, chunk)
    ob = attend(q, k_outs, v_outs, lam_vecs, diff_norm)
    mix = branch_mix(oa, ob, W["w_pa"], W["w_pb"], sig_gm)
    h1 = matmul_residual(mix, W["w_out"], x)
    h2 = moe_dense(h1, norm_ffn, W["w_router"], W["b_router"], W["w_eg"], W["w_eu"], W["w_ed"])
    y = ple_final(h2, p_emb, norm_ple, norm_final, W["w_pg"], W["w_pp"])
    return y, s_new, k_outs[0], v_outs[0]


class _PromptAttention:
    q_dtype = BF16
    kv_extra = [BF16]
    q_scale = DK_B ** -0.5 * LOG2_E

    def __init__(self, batch):
        self.batch = batch

    def __call__(self, q, k_outs, v_outs, lam_vecs, diff_norm):
        return flash_diff_attention(q, k_outs[1], v_outs[1], lam_vecs, diff_norm, self.batch)


class _PagedAttention:
    q_dtype = F32
    kv_extra = []
    q_scale = DK_B ** -0.5

    def __init__(self, batch, cache_k, cache_v, page_table):
        self.batch, self.cache_k, self.cache_v, self.page_table = batch, cache_k, cache_v, page_table

    def __call__(self, q, k_outs, v_outs, lam_vecs, diff_norm):
        b = self.batch
        t = q.shape[0] // b
        out = paged_diff_attention(q.reshape(b, t, -1), k_outs[0].reshape(b, t, -1), v_outs[0].reshape(b, t, -1),
                                   self.cache_k, self.cache_v, self.page_table, lam_vecs, diff_norm)
        return out.reshape(b * t, -1).astype(BF16)


def kernel(x_prompt, x_sample, cache_k, cache_v, state_gla, page_table, p_prompt, p_sample, norm_mix, w_in, w_alpha2, b_alpha, gla_norm, w_proj_a, lambda_q1, lambda_k1, lambda_q2, lambda_k2, diff_norm, w_proj_b, w_out, norm_ffn, w_router_group, b_router_group, w_router_expert, b_router_expert, w_exp_gate, w_exp_up, w_exp_down, norm_ple, w_ple_gate, w_ple_proj, norm_final):
    assert w_in.shape[0] == 1, "one layer"
    bp, tp, d = x_prompt.shape
    bs, ts, _ = x_sample.shape
    n_pool, page = cache_k.shape[1], cache_k.shape[2]
    past_len = page_table.shape[1] * page

    W = _prepare_weights(w_in[0], w_alpha2[0], b_alpha[0], w_proj_a[0], w_proj_b[0], w_out[0],
                         w_router_group[0], b_router_group[0], w_router_expert[0], b_router_expert[0],
                         w_exp_gate[0], w_exp_up[0], w_exp_down[0], w_ple_gate[0], w_ple_proj[0])
    norms = (norm_mix[0], gla_norm[0], diff_norm[0], norm_ffn[0], norm_ple[0], norm_final)
    lam_vecs = jnp.stack([lambda_q1[0], lambda_k1[0], lambda_q2[0], lambda_k2[0]])

    pos_p = jnp.tile(jnp.arange(tp, dtype=jnp.int32), bp)
    yp, sp, kp, vp = _layer(x_prompt.reshape(bp * tp, d), p_prompt[0].reshape(bp * tp, -1), pos_p, bp,
                            min(GLA_CHUNK, tp), None, _PromptAttention(bp), W, norms, lam_vecs, BF16)

    pos_s = jnp.tile(past_len + jnp.arange(ts, dtype=jnp.int32), bs)
    attend_s = _PagedAttention(bs, cache_k.reshape(n_pool, page * H_B * 2, DK_B),
                               cache_v.reshape(n_pool, page * H_B, DV_B), page_table)
    ys, ss, ks, vs = _layer(x_sample.reshape(bs * ts, d), p_sample[0].reshape(bs * ts, -1), pos_s, bs,
                            ts, state_gla[0], attend_s, W, norms, lam_vecs, F32)

    return (yp.reshape(bp, tp, d), ys.reshape(bs, ts, d),
            sp[None], kp.reshape(1, bp, tp, H_B, 2, DK_B), vp.reshape(1, bp, tp, H_B, DV_B),
            ss[None], ks.reshape(1, bs, ts, H_B, 2, DK_B), vs.reshape(1, bs, ts, H_B, DV_B))
```

```python
import functools

import jax
import jax.numpy as jnp
import numpy as np
from jax import lax
from jax.experimental import pallas as pl
from jax.experimental.pallas import tpu as pltpu

F32 = jnp.float32
BF16 = jnp.bfloat16

H_A, DK_A, DV_A = 4, 128, 256
GATE_RANK = 16
GATE_TAU = 16.0
GLA_CHUNK = 64
H_B, DK_B, DV_B = 8, 128, 256
ROPE_THETA = 10000.0
N_GROUPS, EXPERTS_PER_GROUP, D_EXPERT = 4, 8, 256
N_EXPERTS = N_GROUPS * EXPERTS_PER_GROUP
EPS = 1e-6
LAM_INIT = 0.8 - 0.6

LANES = 128
V7X_VMEM_LIMIT = 56 * 1024 * 1024

HB_W = 2 * DK_B
FLASH_TQ, FLASH_TK = 1024, 1024
FLASH_RQ = 128
LOG2_E = 1.4426950408889634
PAGES_PER_STEP = 8
PAGED_CHUNKS = 4
EXPERT_LANE0 = 32
NEG_INF = float("-inf")


def _cparams(*sem):
    return pltpu.CompilerParams(dimension_semantics=sem, vmem_limit_bytes=V7X_VMEM_LIMIT)


def _tile(n, candidates):
    for c in candidates:
        if n % c == 0:
            return c
    return n


def _sigmoid(x):
    return 1.0 / (1.0 + jnp.exp(-x))


def _rms(x, g):
    ms = jnp.mean(x * x, axis=-1, keepdims=True)
    return x * lax.rsqrt(ms + EPS) * g


def _rmsnorm_kernel(x_ref, g_ref, o_ref):
    o_ref[...] = _rms(x_ref[...], g_ref[...]).astype(o_ref.dtype)


def rmsnorm_cast(x, g, out_dtype=BF16):
    m, d = x.shape
    tm = _tile(m, (512, 256, 128, 8))
    return pl.pallas_call(
        _rmsnorm_kernel,
        out_shape=jax.ShapeDtypeStruct((m, d), out_dtype),
        grid=(m // tm,),
        in_specs=[pl.BlockSpec((tm, d), lambda i: (i, 0)), pl.BlockSpec((1, d), lambda i: (0, 0))],
        out_specs=pl.BlockSpec((tm, d), lambda i: (i, 0)),
        compiler_params=_cparams("parallel"),
        name="rmsnorm_cast",
    )(x, g.reshape(1, d))


def _mm_kernel(x_ref, w_ref, *o_refs, act):
    acc = jnp.dot(x_ref[...], w_ref[...], preferred_element_type=F32)
    if act == "sigmoid":
        acc = _sigmoid(acc)
    for o_ref in o_refs:
        o_ref[...] = acc.astype(o_ref.dtype)


def matmul(x, w, out_dtypes, act=None, tn_candidates=(1024, 640, 512, 256, 128), name="matmul"):
    m, k = x.shape
    n = w.shape[1]
    tm = _tile(m, (1024, 512, 256, 128))
    tn = _tile(n, tn_candidates)
    outs = pl.pallas_call(
        functools.partial(_mm_kernel, act=act),
        out_shape=[jax.ShapeDtypeStruct((m, n), dt) for dt in out_dtypes],
        grid=(m // tm, n // tn),
        in_specs=[pl.BlockSpec((tm, k), lambda i, j: (i, 0)), pl.BlockSpec((k, tn), lambda i, j: (0, j))],
        out_specs=[pl.BlockSpec((tm, tn), lambda i, j: (i, j)) for _ in out_dtypes],
        compiler_params=_cparams("parallel", "parallel"),
        name=name,
    )(x, w)
    return outs


def _mm_heads_kernel(x_ref, w_ref, oh_ref, *o_refs):
    acc = jnp.dot(x_ref[...], w_ref[...], preferred_element_type=F32)
    for h in range(H_B):
        oh_ref[:, h, :] = acc[:, h * DV_B:(h + 1) * DV_B]
    for o_ref in o_refs:
        o_ref[...] = acc.astype(o_ref.dtype)


def matmul_heads(x, w, extra_dtypes, name):
    m, k = x.shape
    n = w.shape[1]
    assert n == H_B * DV_B
    tm = _tile(m, (512, 256, 128))
    return pl.pallas_call(
        _mm_heads_kernel,
        out_shape=[jax.ShapeDtypeStruct((m, H_B, DV_B), F32)] + [jax.ShapeDtypeStruct((m, n), dt) for dt in extra_dtypes],
        grid=(m // tm,),
        in_specs=[pl.BlockSpec((tm, k), lambda i: (i, 0)), pl.BlockSpec((k, n), lambda i: (0, 0))],
        out_specs=[pl.BlockSpec((tm, H_B, DV_B), lambda i: (i, 0, 0))]
        + [pl.BlockSpec((tm, n), lambda i: (i, 0)) for _ in extra_dtypes],
        compiler_params=_cparams("parallel"),
        name=name,
    )(x, w)


def _rope(acc, cos, sin):
    half = DK_B // 2
    parts = []
    for c in range(acc.shape[1] // DK_B):
        a = acc[:, c * DK_B:(c + 1) * DK_B]
        parts.append(a * cos + pltpu.roll(a, half, axis=1) * sin)
    return parts


def _mm_rope_rows_kernel(x_ref, w_ref, cos_ref, sin_ref, or_ref, *o_refs):
    acc = jnp.dot(x_ref[...], w_ref[...], preferred_element_type=F32)
    parts = _rope(acc, cos_ref[...], sin_ref[...])
    tm = acc.shape[0]
    groups = len(parts)
    for c, part in enumerate(parts):
        or_ref[pl.ds(c, tm, stride=groups), :] = part
    rot = jnp.concatenate(parts, axis=1)
    for o_ref in o_refs:
        o_ref[...] = rot.astype(o_ref.dtype)


def matmul_rope_rows(x, w, cos, sin, extra_dtypes, name):
    m, k = x.shape
    n = w.shape[1]
    groups = n // DK_B
    tm = _tile(m, (512, 256, 128))
    return pl.pallas_call(
        _mm_rope_rows_kernel,
        out_shape=[jax.ShapeDtypeStruct((m * groups, DK_B), F32)] + [jax.ShapeDtypeStruct((m, n), dt) for dt in extra_dtypes],
        grid=(m // tm,),
        in_specs=[
            pl.BlockSpec((tm, k), lambda i: (i, 0)),
            pl.BlockSpec((k, n), lambda i: (0, 0)),
            pl.BlockSpec((tm, DK_B), lambda i: (i, 0)),
            pl.BlockSpec((tm, DK_B), lambda i: (i, 0)),
        ],
        out_specs=[pl.BlockSpec((tm * groups, DK_B), lambda i: (i, 0))]
        + [pl.BlockSpec((tm, n), lambda i: (i, 0)) for _ in extra_dtypes],
        compiler_params=_cparams("parallel"),
        name=name,
    )(x, w, cos, sin)


def _mm_rope_kernel(x_ref, w_ref, cos_ref, sin_ref, *o_refs, scale):
    acc = jnp.dot(x_ref[...], w_ref[...], preferred_element_type=F32)
    rot = jnp.concatenate(_rope(acc, cos_ref[...], sin_ref[...]), axis=1)
    if scale != 1.0:
        rot = rot * scale
    for o_ref in o_refs:
        o_ref[...] = rot.astype(o_ref.dtype)


def matmul_rope(x, w, cos, sin, out_dtypes, scale=1.0, name="matmul_rope"):
    m, k = x.shape
    n = w.shape[1]
    tm = _tile(m, (1024, 512, 256, 128))
    tn = _tile(n, (1024, 512, 256, 128))
    return pl.pallas_call(
        functools.partial(_mm_rope_kernel, scale=scale),
        out_shape=[jax.ShapeDtypeStruct((m, n), dt) for dt in out_dtypes],
        grid=(m // tm, n // tn),
        in_specs=[
            pl.BlockSpec((tm, k), lambda i, j: (i, 0)),
            pl.BlockSpec((k, tn), lambda i, j: (0, j)),
            pl.BlockSpec((tm, DK_B), lambda i, j: (i, 0)),
            pl.BlockSpec((tm, DK_B), lambda i, j: (i, 0)),
        ],
        out_specs=[pl.BlockSpec((tm, tn), lambda i, j: (i, j)) for _ in out_dtypes],
        compiler_params=_cparams("parallel", "parallel"),
        name=name,
    )(x, w, cos, sin)


def _mix_kernel(oa_ref, ob_ref, wa_ref, wb_ref, sa_ref, sb_ref, o_ref):
    ya = jnp.dot(oa_ref[...], wa_ref[...], preferred_element_type=F32)
    yb = jnp.dot(ob_ref[...], wb_ref[...], preferred_element_type=F32)
    mix = sa_ref[...].astype(F32) * ya + sb_ref[...].astype(F32) * yb
    o_ref[...] = mix.astype(o_ref.dtype)


def branch_mix(oa, ob, w_pa, w_pb, sig_gm):
    m, ka = oa.shape
    kb = ob.shape[1]
    d = w_pa.shape[1]
    tm = _tile(m, (1024, 512, 256, 128))
    tn = _tile(d, (1024, 512, 256, 128))
    nb = d // tn
    return pl.pallas_call(
        _mix_kernel,
        out_shape=jax.ShapeDtypeStruct((m, d), BF16),
        grid=(m // tm, nb),
        in_specs=[
            pl.BlockSpec((tm, ka), lambda i, j: (i, 0)),
            pl.BlockSpec((tm, kb), lambda i, j: (i, 0)),
            pl.BlockSpec((ka, tn), lambda i, j: (0, j)),
            pl.BlockSpec((kb, tn), lambda i, j: (0, j)),
            pl.BlockSpec((tm, tn), lambda i, j: (i, j)),
            pl.BlockSpec((tm, tn), lambda i, j: (i, j + nb)),
        ],
        out_specs=pl.BlockSpec((tm, tn), lambda i, j: (i, j)),
        compiler_params=_cparams("parallel", "parallel"),
        name="branch_mix",
    )(oa, ob, w_pa, w_pb, sig_gm, sig_gm)


def _mm_res_kernel(x_ref, w_ref, r_ref, o_ref):
    o_ref[...] = r_ref[...] + jnp.dot(x_ref[...], w_ref[...], preferred_element_type=F32)


def matmul_residual(x, w, res):
    m, k = x.shape
    n = w.shape[1]
    tm = _tile(m, (1024, 512, 256, 128))
    tn = _tile(n, (1024, 512, 256, 128))
    return pl.pallas_call(
        _mm_res_kernel,
        out_shape=jax.ShapeDtypeStruct((m, n), F32),
        grid=(m // tm, n // tn),
        in_specs=[
            pl.BlockSpec((tm, k), lambda i, j: (i, 0)),
            pl.BlockSpec((k, tn), lambda i, j: (0, j)),
            pl.BlockSpec((tm, tn), lambda i, j: (i, j)),
        ],
        out_specs=pl.BlockSpec((tm, tn), lambda i, j: (i, j)),
        compiler_params=_cparams("parallel", "parallel"),
        name="out_proj_residual",
    )(x, w, res)


def _gla_kernel(q_ref, k_ref, v_ref, r_ref, ga_ref, wal_ref, bal_ref, gn_ref, *rest, has_s0, cp):
    if has_s0:
        s0_ref, o_ref, s_out_ref, st_ref = rest
    else:
        o_ref, s_out_ref, st_ref = rest
    ci = pl.program_id(1)
    c = q_ref.shape[1]

    @pl.when(ci == 0)
    def _init():
        for h in range(H_A):
            if has_s0:
                st_ref[h] = s0_ref[0, h].T
            else:
                st_ref[h] = jnp.zeros(st_ref.shape[1:], F32)

    def padded(x):
        x = x.astype(F32)
        if cp == c:
            return x
        return jnp.concatenate([x, jnp.zeros((cp - c, x.shape[1]), F32)], axis=0)

    row = lax.broadcasted_iota(jnp.int32, (cp, cp), 0)
    col = lax.broadcasted_iota(jnp.int32, (cp, cp), 1)
    causal = row >= col
    tril = causal.astype(F32)
    ga = padded(ga_ref[0]).astype(BF16)
    x = jnp.dot(ga, wal_ref[...], preferred_element_type=F32) + bal_ref[...]
    logf = (jnp.minimum(x, 0.0) - jnp.log(1.0 + jnp.exp(-jnp.abs(x)))) * (1.0 / GATE_TAU)
    if cp != c:
        logf = jnp.where(lax.broadcasted_iota(jnp.int32, logf.shape, 0) < c, logf, 0.0)
    b_all = jnp.dot(tril, logf, precision=lax.Precision.HIGHEST, preferred_element_type=F32)
    outs = []
    for h in range(H_A):
        q = padded(q_ref[0, :, h * DK_A:(h + 1) * DK_A]) * (DK_A ** -0.5)
        k = padded(k_ref[0, :, h * DK_A:(h + 1) * DK_A])
        v = padded(v_ref[0, :, h * DV_A:(h + 1) * DV_A]).astype(BF16)
        b = b_all[:, h * DK_A:(h + 1) * DK_A]
        b_last = b[cp - 1:cp, :]

        qd = (q * jnp.exp(b)).astype(BF16)
        kd = (k * jnp.exp(-b)).astype(BF16)
        ke = (k * jnp.exp(b_last - b)).astype(BF16)

        a = lax.dot_general(qd, kd, (((1,), (1,)), ((), ())), preferred_element_type=F32)
        a = jnp.where(causal, a, 0.0)
        st = st_ref[h]
        o = jnp.dot(a.astype(BF16), v, preferred_element_type=F32)
        o = o + lax.dot_general(qd, st.astype(BF16), (((1,), (1,)), ((), ())), preferred_element_type=F32)
        kv_t = lax.dot_general(v, ke, (((0,), (0,)), ((), ())), preferred_element_type=F32)
        st_ref[h] = st * jnp.exp(b_last) + kv_t

        r = r_ref[0, :, h * DV_A:(h + 1) * DV_A].astype(F32)
        outs.append(_rms(o[:c], gn_ref[...]) * (r * _sigmoid(r)))
    o_ref[0] = jnp.concatenate(outs, axis=1).astype(o_ref.dtype)

    @pl.when(ci == pl.num_programs(1) - 1)
    def _final():
        for h in range(H_A):
            s_out_ref[0, h] = st_ref[h].T


def gla(za, w_alpha2, b_alpha, gla_norm, s0, batch, chunk):
    m = za.shape[0]
    t = m // batch
    n_chunks = t // chunk
    cp = chunk if chunk % 16 == 0 else LANES
    za3 = za.reshape(batch, t, za.shape[1])
    wal = jnp.pad(w_alpha2, ((0, LANES - GATE_RANK), (0, 0))).astype(BF16)
    bal = b_alpha.reshape(1, H_A * DK_A)
    qk_w = H_A * DK_A
    vr_w = H_A * DV_A
    gcol = (2 * qk_w + 2 * vr_w) // LANES
    in_specs = [
        pl.BlockSpec((1, chunk, qk_w), lambda b, c: (b, c, 0)),
        pl.BlockSpec((1, chunk, qk_w), lambda b, c: (b, c, 1)),
        pl.BlockSpec((1, chunk, vr_w), lambda b, c: (b, c, 1)),
        pl.BlockSpec((1, chunk, vr_w), lambda b, c: (b, c, 2)),
        pl.BlockSpec((1, chunk, LANES), lambda b, c: (b, c, gcol)),
        pl.BlockSpec((LANES, qk_w), lambda b, c: (0, 0)),
        pl.BlockSpec((1, qk_w), lambda b, c: (0, 0)),
        pl.BlockSpec((1, DV_A), lambda b, c: (0, 0)),
    ]
    args = [za3, za3, za3, za3, za3, wal, bal, gla_norm.reshape(1, DV_A)]
    if s0 is not None:
        in_specs.append(pl.BlockSpec((1, H_A, DK_A, DV_A), lambda b, c: (b, 0, 0, 0)))
        args.append(s0)
    oa, s_new = pl.pallas_call(
        functools.partial(_gla_kernel, has_s0=s0 is not None, cp=cp),
        out_shape=[
            jax.ShapeDtypeStruct((batch, t, vr_w), BF16 if cp == chunk else F32),
            jax.ShapeDtypeStruct((batch, H_A, DK_A, DV_A), F32),
        ],
        grid=(batch, n_chunks),
        in_specs=in_specs,
        out_specs=[
            pl.BlockSpec((1, chunk, vr_w), lambda b, c: (b, c, 0)),
            pl.BlockSpec((1, H_A, DK_A, DV_A), lambda b, c: (b, 0, 0, 0)),
        ],
        scratch_shapes=[pltpu.VMEM((H_A, DV_A, DK_A), F32)],
        compiler_params=_cparams("parallel", "arbitrary"),
        name="gla",
    )(*args)
    return oa.reshape(m, vr_w).astype(BF16), s_new


def _lambda(lam_ref):
    lv = lam_ref[...]
    s1 = jnp.sum(lv[0:1] * lv[1:2], axis=-1, keepdims=True)
    s2 = jnp.sum(lv[2:3] * lv[3:4], axis=-1, keepdims=True)
    return jnp.exp(s1) - jnp.exp(s2) + LAM_INIT


def _lane_tile(x, width):
    return x if width == LANES else jnp.concatenate([x] * (width // LANES), axis=1)


def _flash_kernel(qi_ref, kj_ref, q_ref, k_ref, v_ref, lam_ref, dn_ref, o_ref,
                  m_ref, l_ref, acc_ref, *, tq, tk, rq):
    step = pl.program_id(2)
    qi = qi_ref[step]
    kj = kj_ref[step]

    @pl.when(kj == 0)
    def _init():
        m_ref[...] = jnp.full_like(m_ref, NEG_INF)
        l_ref[...] = jnp.zeros_like(l_ref)
        acc_ref[...] = jnp.zeros_like(acc_ref)

    def update(masked):
        for r0 in range(0, tq, rq):
            trimmed = masked and tq == tk
            kw = r0 + rq if trimmed else tk
            v = v_ref[0, 0:kw, :]
            if trimmed:
                visible = (lax.broadcasted_iota(jnp.int32, (rq, rq), 1)
                           <= lax.broadcasted_iota(jnp.int32, (rq, rq), 0))
            elif masked:
                q_pos = qi * tq + r0 + lax.broadcasted_iota(jnp.int32, (rq, kw), 0)
                k_pos = kj * tk + lax.broadcasted_iota(jnp.int32, (rq, kw), 1)
                visible = k_pos <= q_pos
            for mi in range(2):
                q = q_ref[0, r0:r0 + rq, mi * DK_B:(mi + 1) * DK_B]
                k = k_ref[0, 0:kw, mi * DK_B:(mi + 1) * DK_B]
                s = lax.dot_general(q, k, (((1,), (1,)), ((), ())), preferred_element_type=F32)
                if trimmed:
                    s_diag = jnp.where(visible, s[:, kw - rq:], NEG_INF)
                    s = s_diag if kw == rq else jnp.concatenate([s[:, :kw - rq], s_diag], axis=1)
                elif masked:
                    s = jnp.where(visible, s, NEG_INF)
                m_prev = m_ref[mi, r0:r0 + rq, :]
                m_new = jnp.maximum(m_prev, jnp.max(s, axis=-1, keepdims=True))
                alpha = jnp.exp2(m_prev - m_new)
                p = jnp.exp2(s - _lane_tile(m_new, kw))
                l_ref[mi, r0:r0 + rq, :] = alpha * l_ref[mi, r0:r0 + rq, :] + jnp.sum(p, axis=-1, keepdims=True)
                pv = jnp.dot(p.astype(BF16), v, preferred_element_type=F32)
                acc_ref[mi, r0:r0 + rq, :] = _lane_tile(alpha, DV_B) * acc_ref[mi, r0:r0 + rq, :] + pv
                m_ref[mi, r0:r0 + rq, :] = m_new

    on_diagonal = (kj + 1) * tk - 1 > qi * tq

    @pl.when(on_diagonal)
    def _masked():
        update(True)

    @pl.when(jnp.logical_not(on_diagonal))
    def _unmasked():
        update(False)

    last_kj = ((qi + 1) * tq - 1) // tk

    @pl.when(kj == last_kj)
    def _final():
        lam = _lambda(lam_ref)
        o = (acc_ref[0] / _lane_tile(l_ref[0], DV_B)
             - lam * (acc_ref[1] / _lane_tile(l_ref[1], DV_B)))
        o_ref[0] = (_rms(o, dn_ref[...]) * (1.0 - LAM_INIT)).astype(o_ref.dtype)


def flash_diff_attention(q, k, v, lam_vecs, diff_norm, batch):
    m = q.shape[0]
    t = m // batch
    tq = min(FLASH_TQ, t)
    tk = min(FLASH_TK, t)
    pairs = [(i, j) for i in range(t // tq) for j in range(((i + 1) * tq - 1) // tk + 1)]
    qi = jnp.asarray(np.array([p[0] for p in pairs], np.int32))
    kj = jnp.asarray(np.array([p[1] for p in pairs], np.int32))
    q3 = q.reshape(batch, t, H_B * HB_W)
    k3 = k.reshape(batch, t, H_B * HB_W)
    v3 = v.reshape(batch, t, H_B * DV_B)
    out = pl.pallas_call(
        functools.partial(_flash_kernel, tq=tq, tk=tk, rq=min(FLASH_RQ, tq)),
        out_shape=jax.ShapeDtypeStruct((batch, t, H_B * DV_B), BF16),
        grid_spec=pltpu.PrefetchScalarGridSpec(
            num_scalar_prefetch=2,
            grid=(batch, H_B, len(pairs)),
            in_specs=[
                pl.BlockSpec((1, tq, HB_W), lambda b, h, s, qi, kj: (b, qi[s], h)),
                pl.BlockSpec((1, tk, HB_W), lambda b, h, s, qi, kj: (b, kj[s], h)),
                pl.BlockSpec((1, tk, DV_B), lambda b, h, s, qi, kj: (b, kj[s], h)),
                pl.BlockSpec((4, DK_B), lambda b, h, s, qi, kj: (0, 0)),
                pl.BlockSpec((1, DV_B), lambda b, h, s, qi, kj: (0, 0)),
            ],
            out_specs=pl.BlockSpec((1, tq, DV_B), lambda b, h, s, qi, kj: (b, qi[s], h)),
            scratch_shapes=[
                pltpu.VMEM((2, tq, LANES), F32), pltpu.VMEM((2, tq, LANES), F32),
                pltpu.VMEM((2, tq, DV_B), F32),
            ],
        ),
        compiler_params=_cparams("parallel", "parallel", "arbitrary"),
        name="flash_diff_attention",
    )(qi, kj, q3, k3, v3, lam_vecs, diff_norm.reshape(1, DV_B))
    return out.reshape(m, H_B * DV_B)


def _rows_by_head(x, col0, width, stride):
    return jnp.concatenate([x[:, h * stride + col0:h * stride + col0 + width] for h in range(H_B)], axis=0)


def _paged_kernel(pt_ref, q_ref, kn_ref, vn_ref, lam_ref, dn_ref, *rest, pps, ppc, t_new):
    k_refs = rest[:pps]
    v_refs = rest[pps:2 * pps]
    o_ref = rest[2 * pps]
    qm, kb, vb, mask_ref, m_ref, l_ref, acc_ref = rest[2 * pps + 1:]
    j = pl.program_id(1)
    hq = H_B * t_new
    rows_pg = v_refs[0].shape[1]

    @pl.when(j == 0)
    def _init():
        q = q_ref[0]
        for m in range(2):
            qm[m] = _rows_by_head(q, m * DK_B, DK_B, HB_W).astype(BF16)
        row = lax.broadcasted_iota(jnp.int32, mask_ref.shape, 0)
        col = lax.broadcasted_iota(jnp.int32, mask_ref.shape, 1)
        mask_ref[...] = jnp.where(col % H_B == row // t_new, 0.0, NEG_INF)
        m_ref[...] = jnp.full_like(m_ref, NEG_INF)
        l_ref[...] = jnp.zeros_like(l_ref)
        acc_ref[...] = jnp.zeros_like(acc_ref)

    def attend(k_maps, v, mask):
        width = v.shape[0]
        s = jnp.concatenate(
            [lax.dot_general(qm[m], k_maps[m], (((1,), (1,)), ((), ())), preferred_element_type=F32) + mask
             for m in range(2)], axis=0)
        m_prev = m_ref[...]
        m_new = jnp.maximum(m_prev, jnp.max(s, axis=-1, keepdims=True))
        alpha = jnp.exp(m_prev - m_new)
        p = jnp.exp(s - _lane_tile(m_new, width))
        l_ref[...] = alpha * l_ref[...] + jnp.sum(p, axis=-1, keepdims=True)
        acc_ref[...] = (_lane_tile(alpha, DV_B) * acc_ref[...]
                        + jnp.dot(p.astype(BF16), v, preferred_element_type=F32))
        m_ref[...] = m_new

    for c0 in range(0, pps, ppc):
        for p in range(c0, c0 + ppc):
            for m in range(2):
                kb[m, p * rows_pg:(p + 1) * rows_pg, :] = (
                    k_refs[p][0, pl.ds(m, rows_pg, stride=2), :].astype(BF16))
            vb[p * rows_pg:(p + 1) * rows_pg, :] = v_refs[p][0].astype(BF16)
        lo, hi = c0 * rows_pg, (c0 + ppc) * rows_pg
        attend([kb[0, lo:hi, :], kb[1, lo:hi, :]], vb[lo:hi, :], mask_ref[...])

    @pl.when(j == pl.num_programs(1) - 1)
    def _final():
        kn = kn_ref[0]
        vn = vn_ref[0]
        k_new = [jnp.concatenate([_rows_by_head(kn, m * DK_B, DK_B, HB_W),
                                  jnp.zeros((LANES - hq, DK_B), F32)], axis=0).astype(BF16) for m in range(2)]
        v_new = jnp.concatenate([_rows_by_head(vn, 0, DV_B, DV_B),
                                 jnp.zeros((LANES - hq, DV_B), F32)], axis=0).astype(BF16)
        row = lax.broadcasted_iota(jnp.int32, (hq, LANES), 0)
        col = lax.broadcasted_iota(jnp.int32, (hq, LANES), 1)
        visible = (col // t_new == row // t_new) & (col % t_new <= row % t_new)
        attend(k_new, v_new, jnp.where(visible, 0.0, NEG_INF))

        lam = _lambda(lam_ref)
        w = acc_ref[...] / _lane_tile(l_ref[...], DV_B)
        heads = []
        for h in range(H_B):
            w1 = w[h * t_new:(h + 1) * t_new]
            w2 = w[hq + h * t_new:hq + (h + 1) * t_new]
            heads.append(_rms(w1 - lam * w2, dn_ref[...]) * (1.0 - LAM_INIT))
        o_ref[0] = jnp.concatenate(heads, axis=1).astype(o_ref.dtype)


def paged_diff_attention(q, k_new, v_new, cache_k, cache_v, page_table, lam_vecs, diff_norm):
    db, t_new, width = q.shape
    n_pages = page_table.shape[1]
    pps = min(PAGES_PER_STEP, n_pages)
    ppc = pps // min(PAGED_CHUNKS, pps)
    hq = H_B * t_new
    rows_pg = cache_v.shape[1]
    assert n_pages % pps == 0 and hq <= LANES and cache_k.shape[1] == 2 * rows_pg

    def k_spec(p):
        return pl.BlockSpec((1, 2 * rows_pg, DK_B), lambda b, j, pt: (pt[b, j * pps + p], 0, 0))

    def v_spec(p):
        return pl.BlockSpec((1, rows_pg, DV_B), lambda b, j, pt: (pt[b, j * pps + p], 0, 0))

    seq_spec = pl.BlockSpec((1, t_new, width), lambda b, j, pt: (b, 0, 0))
    return pl.pallas_call(
        functools.partial(_paged_kernel, pps=pps, ppc=ppc, t_new=t_new),
        out_shape=jax.ShapeDtypeStruct((db, t_new, H_B * DV_B), F32),
        grid_spec=pltpu.PrefetchScalarGridSpec(
            num_scalar_prefetch=1,
            grid=(db, n_pages // pps),
            in_specs=[seq_spec, seq_spec, seq_spec,
                      pl.BlockSpec((4, DK_B), lambda b, j, pt: (0, 0)),
                      pl.BlockSpec((1, DV_B), lambda b, j, pt: (0, 0))]
            + [k_spec(p) for p in range(pps)] + [v_spec(p) for p in range(pps)],
            out_specs=pl.BlockSpec((1, t_new, H_B * DV_B), lambda b, j, pt: (b, 0, 0)),
            scratch_shapes=[
                pltpu.VMEM((2, hq, DK_B), BF16),
                pltpu.VMEM((2, pps * rows_pg, DK_B), BF16),
                pltpu.VMEM((pps * rows_pg, DV_B), BF16),
                pltpu.VMEM((hq, ppc * rows_pg), F32),
                pltpu.VMEM((2 * hq, LANES), F32), pltpu.VMEM((2 * hq, LANES), F32),
                pltpu.VMEM((2 * hq, DV_B), F32),
            ],
        ),
        compiler_params=_cparams("parallel", "arbitrary"),
        name="paged_diff_attention",
    )(page_table, q, k_new, v_new, lam_vecs, diff_norm.reshape(1, DV_B),
      *([cache_k] * pps), *([cache_v] * pps))


def _split_bf16(x):
    hi = x.astype(BF16)
    lo = (x - hi.astype(F32)).astype(BF16)
    return hi, lo


def _route(logits):
    lane = lax.broadcasted_iota(jnp.int32, logits.shape, 1)
    lane_f = lane.astype(F32)
    big = float(LANES)
    lg = jnp.where(lane < N_GROUPS, logits, NEG_INF)
    mg = jnp.max(lg, axis=-1, keepdims=True)
    g_idx = jnp.min(jnp.where(lg == mg, lane_f, big), axis=-1, keepdims=True)
    pg_top = 1.0 / jnp.sum(jnp.exp(lg - mg), axis=-1, keepdims=True)
    lo = EXPERT_LANE0 + g_idx * EXPERTS_PER_GROUP
    sel = (lane_f >= lo) & (lane_f < lo + EXPERTS_PER_GROUP)
    le = jnp.where(sel, logits, NEG_INF)
    m1 = jnp.max(le, axis=-1, keepdims=True)
    i1 = jnp.min(jnp.where(le == m1, lane_f, big), axis=-1, keepdims=True)
    le2 = jnp.where(lane_f == i1, NEG_INF, le)
    m2 = jnp.max(le2, axis=-1, keepdims=True)
    i2 = jnp.min(jnp.where(le2 == m2, lane_f, big), axis=-1, keepdims=True)
    e2 = jnp.exp(m2 - m1)
    w1 = pg_top / (1.0 + e2)
    w2 = pg_top * e2 / (1.0 + e2)
    return jnp.where(lane_f == i1, w1, 0.0) + jnp.where(lane_f == i2, w2, 0.0)


def _moe_kernel(h_ref, g_ref, wr_ref, br_ref, wg_ref, wu_ref, wd_ref, o_ref, x_ref, gate_ref):
    e = pl.program_id(1)

    @pl.when(e == 0)
    def _init():
        h = h_ref[...]
        x = _rms(h, g_ref[...])
        x_ref[...] = x.astype(BF16)
        x_hi, x_lo = _split_bf16(x)
        w_hi, w_lo = _split_bf16(wr_ref[...])
        logits = (jnp.dot(x_hi, w_hi, preferred_element_type=F32)
                  + jnp.dot(x_lo, w_hi, preferred_element_type=F32)
                  + jnp.dot(x_hi, w_lo, preferred_element_type=F32)) + br_ref[...]
        gate_ref[...] = _route(logits)
        o_ref[...] = h

    x = x_ref[...]
    hg = jnp.dot(x, wg_ref[0], preferred_element_type=F32)
    hu = jnp.dot(x, wu_ref[0], preferred_element_type=F32)
    lane = lax.broadcasted_iota(jnp.int32, gate_ref.shape, 1)
    gcol = jnp.sum(jnp.where(lane == e + EXPERT_LANE0, gate_ref[...], 0.0), axis=-1, keepdims=True)
    act = (hg * _sigmoid(hg)) * hu * gcol
    o_ref[...] += jnp.dot(act.astype(BF16), wd_ref[0], preferred_element_type=F32)


def moe_dense(h, norm_ffn, w_router, b_router, w_gate, w_up, w_down):
    m, d = h.shape
    tm = _tile(m, (1024, 512, 256, 128))
    return pl.pallas_call(
        _moe_kernel,
        out_shape=jax.ShapeDtypeStruct((m, d), F32),
        grid=(m // tm, N_EXPERTS),
        in_specs=[
            pl.BlockSpec((tm, d), lambda i, e: (i, 0)),
            pl.BlockSpec((1, d), lambda i, e: (0, 0)),
            pl.BlockSpec((d, LANES), lambda i, e: (0, 0)),
            pl.BlockSpec((1, LANES), lambda i, e: (0, 0)),
            pl.BlockSpec((1, d, D_EXPERT), lambda i, e: (e, 0, 0)),
            pl.BlockSpec((1, d, D_EXPERT), lambda i, e: (e, 0, 0)),
            pl.BlockSpec((1, D_EXPERT, d), lambda i, e: (e, 0, 0)),
        ],
        out_specs=pl.BlockSpec((tm, d), lambda i, e: (i, 0)),
        scratch_shapes=[pltpu.VMEM((tm, d), BF16), pltpu.VMEM((tm, LANES), F32)],
        compiler_params=_cparams("parallel", "arbitrary"),
        name="moe_dense",
    )(h, norm_ffn.reshape(1, d), w_router, b_router, w_gate, w_up, w_down)


def _ple_kernel(h_ref, p_ref, gp_ref, gf_ref, wg_ref, wp_ref, o_ref):
    h = h_ref[...]
    xn = _rms(h, gp_ref[...]).astype(BF16)
    gate = _sigmoid(jnp.dot(xn, wg_ref[...], preferred_element_type=F32))
    pp = jnp.dot(p_ref[...].astype(BF16), wp_ref[...], preferred_element_type=F32)
    o_ref[...] = _rms(h + pp * gate, gf_ref[...])


def ple_final(h, p, norm_ple, norm_final, w_gate, w_proj):
    m, d = h.shape
    pd = p.shape[1]
    tm = _tile(m, (512, 256, 128))
    return pl.pallas_call(
        _ple_kernel,
        out_shape=jax.ShapeDtypeStruct((m, d), F32),
        grid=(m // tm,),
        in_specs=[
            pl.BlockSpec((tm, d), lambda i: (i, 0)),
            pl.BlockSpec((tm, pd), lambda i: (i, 0)),
            pl.BlockSpec((1, d), lambda i: (0, 0)),
            pl.BlockSpec((1, d), lambda i: (0, 0)),
            pl.BlockSpec((d, d), lambda i: (0, 0)),
            pl.BlockSpec((pd, d), lambda i: (0, 0)),
        ],
        out_specs=pl.BlockSpec((tm, d), lambda i: (i, 0)),
        compiler_params=_cparams("parallel"),
        name="ple_final",
    )(h, p, norm_ple.reshape(1, d), norm_final.reshape(1, d), w_gate, w_proj)


def _rope_tables(pos):
    half = DK_B // 2
    inv = ROPE_THETA ** (-jnp.arange(half, dtype=F32) / half)
    ang = pos.astype(F32)[:, None] * inv[None, :]
    cos = jnp.cos(ang)
    sin = jnp.sin(ang)
    return jnp.concatenate([cos, cos], axis=-1), jnp.concatenate([-sin, sin], axis=-1)


def _prepare_weights(w_in, w_alpha2, b_alpha, w_proj_a, w_proj_b, w_out, w_router_group, b_router_group,
                     w_router_expert, b_router_expert, w_exp_gate, w_exp_up, w_exp_down, w_ple_gate, w_ple_proj):
    d = w_in.shape[0]
    sizes = [H_A * DK_A, H_A * DK_A, H_A * DV_A, H_A * DV_A, GATE_RANK,
             H_B * HB_W, H_B * HB_W, H_B * DV_B, 2 * d]
    off = np.concatenate([[0], np.cumsum(sizes)])
    w_gla = jnp.concatenate(
        [w_in[:, off[0]:off[4]], jnp.pad(w_in[:, off[4]:off[5]], ((0, 0), (0, LANES - GATE_RANK)))], axis=1)
    w_router = jnp.zeros((d, LANES), F32)
    w_router = w_router.at[:, :N_GROUPS].set(w_router_group)
    w_router = w_router.at[:, EXPERT_LANE0:EXPERT_LANE0 + N_EXPERTS].set(
        w_router_expert.transpose(1, 0, 2).reshape(d, N_EXPERTS))
    b_router = jnp.zeros((1, LANES), F32)
    b_router = b_router.at[0, :N_GROUPS].set(b_router_group)
    b_router = b_router.at[0, EXPERT_LANE0:EXPERT_LANE0 + N_EXPERTS].set(b_router_expert.reshape(N_EXPERTS))
    return dict(
        w_gla=w_gla.astype(BF16),
        w_q=w_in[:, off[5]:off[6]].astype(BF16),
        w_k=w_in[:, off[6]:off[7]].astype(BF16),
        w_v=w_in[:, off[7]:off[8]].astype(BF16),
        w_gm=w_in[:, off[8]:off[9]].astype(BF16),
        w_alpha2=w_alpha2, b_alpha=b_alpha,
        w_pa=w_proj_a.astype(BF16), w_pb=w_proj_b.astype(BF16), w_out=w_out.astype(BF16),
        w_router=w_router, b_router=b_router,
        w_eg=w_exp_gate.astype(BF16), w_eu=w_exp_up.astype(BF16), w_ed=w_exp_down.astype(BF16),
        w_pg=w_ple_gate.astype(BF16), w_pp=w_ple_proj.astype(BF16),
    )


def _layer(x, p_emb, pos, batch, chunk, s0, attend, W, norms, lam_vecs, za_dtype):
    norm_mix, gla_norm, diff_norm, norm_ffn, norm_ple, norm_final = norms
    hn = rmsnorm_cast(x, norm_mix)
    cos, sin = _rope_tables(pos)
    (za,) = matmul(hn, W["w_gla"], [za_dtype], name="in_proj_gla")
    (q,) = matmul_rope(hn, W["w_q"], cos, sin, [attend.q_dtype], scale=attend.q_scale, name="in_proj_q")
    k_outs = matmul_rope_rows(hn, W["w_k"], cos, sin, attend.kv_extra, name="in_proj_k")
    v_outs = matmul_heads(hn, W["w_v"], attend.kv_extra, name="in_proj_v")
    (sig_gm,) = matmul(hn, W["w_gm"], [BF16], act="sigmoid", name="in_proj_gates")

    oa, s_new = gla(za, W["w_alpha2"], W["b_alpha"], gla_norm, s0, batch, chunk)
    ob = attend(q, k_outs, v_outs, lam_vecs, diff_norm)
    mix = branch_mix(oa, ob, W["w_pa"], W["w_pb"], sig_gm)
    h1 = matmul_residual(mix, W["w_out"], x)
    h2 = moe_dense(h1, norm_ffn, W["w_router"], W["b_router"], W["w_eg"], W["w_eu"], W["w_ed"])
    y = ple_final(h2, p_emb, norm_ple, norm_final, W["w_pg"], W["w_pp"])
    return y, s_new, k_outs[0], v_outs[0]


class _PromptAttention:
    q_dtype = BF16
    kv_extra = [BF16]
    q_scale = DK_B ** -0.5 * LOG2_E

    def __init__(self, batch):
        self.batch = batch

    def __call__(self, q, k_outs, v_outs, lam_vecs, diff_norm):
        return flash_diff_attention(q, k_outs[1], v_outs[1], lam_vecs, diff_norm, self.batch)


class _PagedAttention:
    q_dtype = F32
    kv_extra = [F32]
    q_scale = DK_B ** -0.5

    def __init__(self, batch, cache_k, cache_v, page_table):
        self.batch, self.cache_k, self.cache_v, self.page_table = batch, cache_k, cache_v, page_table

    def __call__(self, q, k_outs, v_outs, lam_vecs, diff_norm):
        b = self.batch
        t = q.shape[0] // b
        out = paged_diff_attention(q.reshape(b, t, -1), k_outs[1].reshape(b, t, -1), v_outs[1].reshape(b, t, -1),
                                   self.cache_k, self.cache_v, self.page_table, lam_vecs, diff_norm)
        return out.reshape(b * t, -1).astype(BF16)


def kernel(x_prompt, x_sample, cache_k, cache_v, state_gla, page_table, p_prompt, p_sample, norm_mix, w_in, w_alpha2, b_alpha, gla_norm, w_proj_a, lambda_q1, lambda_k1, lambda_q2, lambda_k2, diff_norm, w_proj_b, w_out, norm_ffn, w_router_group, b_router_group, w_router_expert, b_router_expert, w_exp_gate, w_exp_up, w_exp_down, norm_ple, w_ple_gate, w_ple_proj, norm_final):
    assert w_in.shape[0] == 1, "one layer"
    bp, tp, d = x_prompt.shape
    bs, ts, _ = x_sample.shape
    n_pool, page = cache_k.shape[1], cache_k.shape[2]
    past_len = page_table.shape[1] * page

    W = _prepare_weights(w_in[0], w_alpha2[0], b_alpha[0], w_proj_a[0], w_proj_b[0], w_out[0],
                         w_router_group[0], b_router_group[0], w_router_expert[0], b_router_expert[0],
                         w_exp_gate[0], w_exp_up[0], w_exp_down[0], w_ple_gate[0], w_ple_proj[0])
    norms = (norm_mix[0], gla_norm[0], diff_norm[0], norm_ffn[0], norm_ple[0], norm_final)
    lam_vecs = jnp.stack([lambda_q1[0], lambda_k1[0], lambda_q2[0], lambda_k2[0]])

    pos_p = jnp.tile(jnp.arange(tp, dtype=jnp.int32), bp)
    yp, sp, kp, vp = _layer(x_prompt.reshape(bp * tp, d), p_prompt[0].reshape(bp * tp, -1), pos_p, bp,
                            min(GLA_CHUNK, tp), None, _PromptAttention(bp), W, norms, lam_vecs, BF16)

    pos_s = jnp.tile(past_len + jnp.arange(ts, dtype=jnp.int32), bs)
    attend_s = _PagedAttention(bs, cache_k.reshape(n_pool, page * H_B * 2, DK_B),
                               cache_v.reshape(n_pool, page * H_B, DV_B), page_table)
    ys, ss, ks, vs = _layer(x_sample.reshape(bs * ts, d), p_sample[0].reshape(bs * ts, -1), pos_s, bs,
                            ts, state_gla[0], attend_s, W, norms, lam_vecs, F32)

    return (yp.reshape(bp, tp, d), ys.reshape(bs, ts, d),
            sp[None], kp.reshape(1, bp, tp, H_B, 2, DK_B), vp.reshape(1, bp, tp, H_B, DV_B),
            ss[None], ks.reshape(1, bs, ts, H_B, 2, DK_B), vs.reshape(1, bs, ts, H_B, DV_B))
```
